```python
import math
import jax, jax.numpy as jnp
from jax import lax
import numpy as np

D_MODEL = 1024
BATCH = 16
SEQ = 4096
DEPTH = 2

GLA_HEADS = 4
GLA_DK = 64
GLA_DV = 128
GLA_GATE_RANK = 16
GLA_GATE_NORM = 16.0
GLA_CHUNK = 64
DSA_HEADS = 8
DSA_DH = 64
IDX_HEADS = 8
IDX_DIM = 64
TOPK_MAX = 256
Q_BLOCK = 128
REL_BUCKETS = 32
REL_MAX_DIST = 128
D_FF = 4 * D_MODEL
PLE_DIM = 256
EPS = 1e-6

GLA_QK = GLA_HEADS * GLA_DK
GLA_V = GLA_HEADS * GLA_DV
DSA_W = DSA_HEADS * DSA_DH
IDX_Q = IDX_HEADS * IDX_DIM
SPLITS = (GLA_QK, GLA_QK, GLA_V, GLA_V, GLA_GATE_RANK, DSA_W, DSA_W, DSA_W,
          IDX_Q, IDX_DIM, IDX_HEADS, D_MODEL, D_MODEL)
N_IN = sum(SPLITS)
SPLIT_POINTS = tuple(int(v) for v in np.cumsum(SPLITS)[:-1])

kernel_name = "hybrid_gla_dsa_griffin_block"


def rms_norm(x, gain):
    xf = x.astype(jnp.float32)
    y = xf * lax.rsqrt(jnp.mean(xf * xf, axis=-1, keepdims=True) + EPS)
    return (y * gain.astype(jnp.float32)).astype(x.dtype)


def rel_bucket(dist):
    max_exact = REL_BUCKETS // 2
    d = jnp.maximum(dist, 1).astype(jnp.float32)
    large = max_exact + (jnp.log(d / max_exact) / math.log(REL_MAX_DIST / max_exact)
                         * (REL_BUCKETS - max_exact)).astype(jnp.int32)
    large = jnp.minimum(large, REL_BUCKETS - 1)
    return jnp.where(dist < max_exact, dist, large)


def gla_chunked(q, k, v, g_log):
    B, S, H, _ = q.shape
    dv = v.shape[-1]
    n = S // GLA_CHUNK

    def to_chunks(a):
        return a.reshape(B, n, GLA_CHUNK, H, a.shape[-1]).transpose(1, 0, 3, 2, 4).astype(jnp.float32)

    causal = jnp.tril(jnp.ones((GLA_CHUNK, GLA_CHUNK), dtype=bool))

    def step(state, inp):
        qb, kb, vb, gb = inp
        b = jnp.cumsum(gb, axis=2)
        inter = jnp.einsum('bhcd,bhde->bhce', qb * jnp.exp(b), state)
        diff = b[:, :, :, None, :] - b[:, :, None, :, :]
        decay = jnp.exp(jnp.where(causal[:, :, None], diff, -jnp.inf))
        attn = jnp.einsum('bhid,bhjd,bhijd->bhij', qb, kb, decay)
        intra = jnp.einsum('bhij,bhje->bhie', attn, vb)
        b_last = b[:, :, -1:, :]
        state = state * jnp.exp(b_last[:, :, 0, :])[..., None] + jnp.einsum(
            'bhcd,bhce->bhde', kb * jnp.exp(b_last - b), vb)
        return state, inter + intra

    state0 = jnp.zeros((B, H, q.shape[-1], dv), jnp.float32)
    _, out = lax.scan(step, state0, (to_chunks(q), to_chunks(k), to_chunks(v), to_chunks(g_log)))
    return out.transpose(1, 0, 3, 2, 4).reshape(B, S, H, dv).astype(q.dtype)


def dsa_attention(q, k, v, qi, ki, wi, rel_bias):
    B, S, H, dh = q.shape
    nb = S // Q_BLOCK
    topk = min(TOPK_MAX, S // 4)
    key_pos = jnp.arange(S, dtype=jnp.int32)
    idx_scale = (IDX_HEADS ** -0.5) * (IDX_DIM ** -0.5)
    ki32 = ki.astype(jnp.float32)

    def blocks(a):
        return a.reshape((B, nb, Q_BLOCK) + a.shape[2:]).swapaxes(0, 1)

    def one_block(inp):
        qb, qib, wib, blk = inp
        qpos = blk * Q_BLOCK + jnp.arange(Q_BLOCK, dtype=jnp.int32)
        visible = key_pos[None, :] <= qpos[:, None]
        rel = jax.nn.relu(jnp.einsum('bqhd,bsd->bqhs', qib.astype(jnp.float32), ki32))
        score = jnp.einsum('bqh,bqhs->bqs', wib.astype(jnp.float32), rel) * idx_scale
        score = jnp.where(visible[None], score, -jnp.inf)
        _, idx = lax.top_k(score, topk)
        k_sel = jax.vmap(lambda kk, ii: kk[ii])(k, idx)
        v_sel = jax.vmap(lambda vv, ii: vv[ii])(v, idx)
        dist = qpos[None, :, None] - idx
        valid = dist >= 0
        bias = rel_bias[rel_bucket(jnp.maximum(dist, 0))].astype(jnp.float32)
        logits = jnp.einsum('bqhd,bqkhd->bqhk', qb.astype(jnp.float32), k_sel.astype(jnp.float32)) \
            * (DSA_DH ** -0.5) + bias.transpose(0, 1, 3, 2)
        logits = jnp.where(valid[:, :, None, :], logits, -jnp.inf)
        probs = jax.nn.softmax(logits, axis=-1)
        out = jnp.einsum('bqhk,bqkhd->bqhd', probs, v_sel.astype(jnp.float32))
        return out.astype(q.dtype)

    outs = lax.map(one_block, (blocks(q), blocks(qi), blocks(wi), jnp.arange(nb, dtype=jnp.int32)))
    return outs.swapaxes(0, 1).reshape(B, S, H, dh)


def setup_inputs(seed: int = 0) -> dict:
    key = jax.random.key(seed)
    ks = jax.random.split(key, 20)
    f32 = jnp.float32

    def w(k, shape, fan_in):
        return jax.random.normal(k, shape, f32) * (fan_in ** -0.5)

    def gain(k, n):
        return 1.0 + 0.05 * jax.random.normal(k, (DEPTH, n), f32)

    return {
        "x": jax.random.normal(ks[0], (BATCH, SEQ, D_MODEL), f32),
        "p": jax.random.normal(ks[1], (DEPTH, BATCH, SEQ, PLE_DIM), f32),
        "rel_bias": 0.5 * jax.random.normal(ks[2], (REL_BUCKETS, DSA_HEADS), f32),
        "ln_mix_pre": gain(ks[3], D_MODEL),
        "ln_mix_post": gain(ks[4], D_MODEL),
        "ln_mlp_pre": gain(ks[5], D_MODEL),
        "ln_mlp_post": gain(ks[6], D_MODEL),
        "ln_ple_post": gain(ks[7], D_MODEL),
        "w_in": w(ks[8], (DEPTH, D_MODEL, N_IN), D_MODEL),
        "gla_gate_w2": w(ks[9], (DEPTH, GLA_GATE_RANK, GLA_QK), GLA_GATE_RANK),
        "gla_gate_b": 0.1 * jax.random.normal(ks[10], (DEPTH, GLA_QK), f32),
        "gla_norm": gain(ks[11], GLA_DV),
        "w_branch_a": w(ks[12], (DEPTH, GLA_V, D_MODEL), GLA_V),
        "w_branch_b": w(ks[13], (DEPTH, DSA_W, D_MODEL), DSA_W),
        "w_out": w(ks[14], (DEPTH, D_MODEL, D_MODEL), D_MODEL),
        "w_mlp_in": w(ks[15], (DEPTH, D_MODEL, D_FF), D_MODEL),
        "w_mlp_out": w(ks[16], (DEPTH, D_FF, D_MODEL), D_FF),
        "w_ple": w(ks[17], (DEPTH, PLE_DIM, D_MODEL), PLE_DIM),
        "w_ple_gate": w(ks[18], (DEPTH, D_MODEL, D_MODEL), D_MODEL),
    }


def reference(x, p, rel_bias, ln_mix_pre, ln_mix_post, ln_mlp_pre, ln_mlp_post, ln_ple_post,
              w_in, gla_gate_w2, gla_gate_b, gla_norm, w_branch_a, w_branch_b, w_out,
              w_mlp_in, w_mlp_out, w_ple, w_ple_gate):
    B, S, _ = x.shape
    h = x
    for i in range(DEPTH):
        u = rms_norm(h, ln_mix_pre[i])
        z = u @ w_in[i]
        (gq, gk, gv, gg, glr, dq, dk, dv, iq, ik, iw, gate_a, gate_b) = jnp.split(
            z, SPLIT_POINTS, axis=-1)

        g_log = jax.nn.log_sigmoid((glr @ gla_gate_w2[i] + gla_gate_b[i]).astype(jnp.float32)) / GLA_GATE_NORM
        o_a = gla_chunked(gq.reshape(B, S, GLA_HEADS, GLA_DK) * (GLA_DK ** -0.5),
                          gk.reshape(B, S, GLA_HEADS, GLA_DK),
                          gv.reshape(B, S, GLA_HEADS, GLA_DV),
                          g_log.reshape(B, S, GLA_HEADS, GLA_DK))
        o_a = rms_norm(o_a, gla_norm[i]) * jax.nn.silu(gg.reshape(B, S, GLA_HEADS, GLA_DV))
        y_a = o_a.reshape(B, S, GLA_V) @ w_branch_a[i]

        o_b = dsa_attention(dq.reshape(B, S, DSA_HEADS, DSA_DH),
                            dk.reshape(B, S, DSA_HEADS, DSA_DH),
                            dv.reshape(B, S, DSA_HEADS, DSA_DH),
                            iq.reshape(B, S, IDX_HEADS, IDX_DIM), ik, iw, rel_bias)
        y_b = o_b.reshape(B, S, DSA_W) @ w_branch_b[i]

        mixed = jax.nn.sigmoid(gate_a) * y_a + jax.nn.sigmoid(gate_b) * y_b
        h = h + rms_norm(mixed @ w_out[i], ln_mix_post[i])

        um = rms_norm(h, ln_mlp_pre[i])
        f = jnp.square(jax.nn.relu(um @ w_mlp_in[i])) @ w_mlp_out[i]
        h = h + rms_norm(f, ln_mlp_post[i])

        e = (p[i] @ w_ple[i]) * jax.nn.sigmoid(h @ w_ple_gate[i])
        h = h + rms_norm(e, ln_ple_post[i])
    return h
```

```python
import functools
import math

import numpy as np
import jax
import jax.numpy as jnp
from jax import lax
from jax.experimental import pallas as pl
from jax.experimental.pallas import tpu as pltpu

GLA_HEADS = 4
GLA_DK = 64
GLA_DV = 128
GLA_GATE_RANK = 16
GLA_GATE_NORM = 16.0
GLA_CHUNK = 64
DSA_HEADS = 8
DSA_DH = 64
IDX_HEADS = 8
IDX_DIM = 64
TOPK_MAX = 256
REL_BUCKETS = 32
REL_MAX_DIST = 128
EPS = 1e-6

GLA_QK = GLA_HEADS * GLA_DK
GLA_V = GLA_HEADS * GLA_DV
DSA_W = DSA_HEADS * DSA_DH
IDX_Q = IDX_HEADS * IDX_DIM

LANES = 128
SUBLANES = 8
VMEM_LIMIT_BYTES = 56 * 1024 * 1024

TOKEN_TILE = 512
GLA_STEP = 256
DSA_QB = 256
FF_CHUNK = 512

INT_MIN = np.int32(-2 ** 31)
NEG_BIG = -1e30

bf16 = jnp.bfloat16
f32 = jnp.float32


def _const_spec(shape):
    nd = len(shape)
    return pl.BlockSpec(shape, lambda *_: (0,) * nd, pipeline_mode=pl.Buffered(1))


def _rms(xf, gain):
    return xf * lax.rsqrt(jnp.mean(xf * xf, axis=-1, keepdims=True) + EPS) * gain


def _dot(a, b):
    return jnp.dot(a, b, preferred_element_type=f32)


def _dot_nt(a, b):
    return lax.dot_general(a, b, (((1,), (1,)), ((), ())), preferred_element_type=f32)


def _dot_tn(a, b):
    return lax.dot_general(a, b, (((0,), (0,)), ((), ())), preferred_element_type=f32)


_P_GLA = (0, 1536)
_P_DQK = (1536, 2560)
_P_IQ = (2560, 3072)
_P_IKK = (3072, 3200)
_P_GATE = (3200, 5248)
_P_GLR = (5248, 5376)
_P_IW = (5376, 5504)
_P_TOTAL = 5504


def _proj_kernel(h_ref, gain_ref, w_ref, wvt_ref, w2_ref, gb_ref,
                 gla_ref, glog_ref, dqk_ref, vt_ref, iq_ref, ikk_ref, iw_ref, gate_ref):
    u = _rms(h_ref[...], gain_ref[...]).astype(bf16)

    def proj(span):
        return _dot(u, w_ref[:, span[0]:span[1]])

    gla_ref[...] = proj(_P_GLA).astype(bf16)
    dqk_ref[...] = proj(_P_DQK).astype(bf16)
    iq_ref[...] = proj(_P_IQ).astype(bf16)
    ikk_ref[...] = proj(_P_IKK).astype(bf16)
    gate_ref[...] = proj(_P_GATE).astype(bf16)
    iw_ref[...] = proj(_P_IW)
    vt_ref[...] = _dot_nt(wvt_ref[...], u).astype(bf16)
    glr = proj(_P_GLR).astype(bf16)
    x = _dot(glr, w2_ref[...]) + gb_ref[...]
    ls = jnp.minimum(x, 0.0) - jnp.log(1.0 + jnp.exp(-jnp.abs(x)))
    glog_ref[...] = ls * (1.0 / GLA_GATE_NORM)


def _proj(h2d, gain, w_all, w_vt, w2p, gb, B, S):
    T, D = h2d.shape
    tm = min(TOKEN_TILE, S)
    nS = S // tm
    tok = lambda n: pl.BlockSpec((tm, n), lambda t: (t, 0))
    out_shape = (
        jax.ShapeDtypeStruct((T, 1536), bf16),
        jax.ShapeDtypeStruct((T, GLA_QK), f32),
        jax.ShapeDtypeStruct((T, 2 * DSA_W), bf16),
        jax.ShapeDtypeStruct((B, DSA_W, S), bf16),
        jax.ShapeDtypeStruct((T, IDX_Q), bf16),
        jax.ShapeDtypeStruct((T, LANES), bf16),
        jax.ShapeDtypeStruct((T, LANES), f32),
        jax.ShapeDtypeStruct((T, 2 * D), bf16),
    )
    out_specs = (
        tok(1536), tok(GLA_QK), tok(2 * DSA_W),
        pl.BlockSpec((None, DSA_W, tm), lambda t: (t // nS, 0, t % nS)),
        tok(IDX_Q), tok(LANES), tok(LANES), tok(2 * D),
    )
    return pl.pallas_call(
        _proj_kernel,
        grid=(T // tm,),
        in_specs=[tok(D), _const_spec((1, D)), _const_spec(w_all.shape), _const_spec(w_vt.shape),
                  _const_spec(w2p.shape), _const_spec((1, GLA_QK))],
        out_specs=out_specs,
        out_shape=out_shape,
        compiler_params=pltpu.CompilerParams(dimension_semantics=("arbitrary",),
                                             vmem_limit_bytes=VMEM_LIMIT_BYTES),
        name="proj",
    )(h2d, gain, w_all, w_vt, w2p, gb)


def _gla_kernel(x_ref, glog_ref, gn_ref, o_ref, state_ref):
    @pl.when(pl.program_id(1) == 0)
    def _():
        state_ref[...] = jnp.zeros_like(state_ref)

    C = GLA_CHUNK
    row = lax.broadcasted_iota(jnp.int32, (C, C), 0)
    col = lax.broadcasted_iota(jnp.int32, (C, C), 1)
    causal = col <= row
    tril = causal.astype(f32)
    gn = gn_ref[...]

    for c in range(GLA_STEP // C):
        rows = slice(c * C, (c + 1) * C)
        q = x_ref[rows, 0:GLA_QK].astype(f32)
        k = x_ref[rows, GLA_QK:2 * GLA_QK].astype(f32)
        g = glog_ref[rows, :]
        b = jnp.dot(tril, g, preferred_element_type=f32, precision=lax.Precision.HIGHEST)
        b_last = b[C - 1:C, :]
        anchor = 0.5 * b_last
        q_inter = (q * jnp.exp(b)).astype(bf16)
        q_s = (q * jnp.exp(b - anchor)).astype(bf16)
        k_s = (k * jnp.exp(anchor - b)).astype(bf16)
        k_d = (k * jnp.exp(b_last - b)).astype(bf16)
        decay = jnp.exp(b_last)
        state = state_ref[...]
        state_b = state.astype(bf16)
        new_cols = []
        for h in range(GLA_HEADS):
            hs = slice(h * GLA_DK, (h + 1) * GLA_DK)
            vs = slice(2 * GLA_QK + h * GLA_DV, 2 * GLA_QK + (h + 1) * GLA_DV)
            gs = slice(2 * GLA_QK + GLA_V + h * GLA_DV, 2 * GLA_QK + GLA_V + (h + 1) * GLA_DV)
            v = x_ref[rows, vs]
            attn = jnp.where(causal, _dot_nt(q_s[:, hs], k_s[:, hs]), 0.0).astype(bf16)
            o = _dot(attn, v) + _dot_nt(q_inter[:, hs], state_b[:, hs])
            new_cols.append(_dot_tn(v, k_d[:, hs]))
            gg = x_ref[rows, gs].astype(f32)
            o_ref[rows, h * GLA_DV:(h + 1) * GLA_DV] = (
                _rms(o, gn) * (gg * jax.nn.sigmoid(gg))).astype(o_ref.dtype)
        state_ref[...] = state * decay + jnp.concatenate(new_cols, axis=1)


def _gla(gla_in, glog, gn, B, S):
    T = gla_in.shape[0]
    step = GLA_STEP
    nS = S // step
    return pl.pallas_call(
        _gla_kernel,
        grid=(B, nS),
        in_specs=[pl.BlockSpec((step, 1536), lambda b, s: (b * nS + s, 0)),
                  pl.BlockSpec((step, GLA_QK), lambda b, s: (b * nS + s, 0)),
                  _const_spec((1, GLA_DV))],
        out_specs=pl.BlockSpec((step, GLA_V), lambda b, s: (b * nS + s, 0)),
        out_shape=jax.ShapeDtypeStruct((T, GLA_V), bf16),
        scratch_shapes=[pltpu.VMEM((GLA_DV, GLA_QK), f32)],
        compiler_params=pltpu.CompilerParams(dimension_semantics=("arbitrary", "arbitrary"),
                                             vmem_limit_bytes=VMEM_LIMIT_BYTES),
        name="gla",
    )(gla_in, glog, gn)


def _sortable(x):
    bits = lax.bitcast_convert_type(x + 0.0, jnp.int32)
    return bits ^ ((bits >> 31) & jnp.int32(0x7FFFFFFF))


def _dsa_kernel(topk, pos_bits, q_ref, k_ref, vt_ref, iq_ref, ikk_ref, iw_ref, bias_ref,
                o_ref, keys_ref, madd_ref, acc_ref):
    QB = DSA_QB
    G = QB // SUBLANES
    qi = pl.program_id(1)
    n_tiles = qi + 1
    idx_scale = (IDX_HEADS ** -0.5) * (IDX_DIM ** -0.5)

    lane_half = lax.broadcasted_iota(jnp.int32, (QB, LANES), 1) // DSA_DH
    key_j = lax.broadcasted_iota(jnp.int32, (QB, QB), 0)
    qry_i = lax.broadcasted_iota(jnp.int32, (QB, QB), 1)
    tri = key_j <= qry_i

    def head_operand(ref, h):
        pair = ref[:, (h // 2) * LANES:(h // 2 + 1) * LANES]
        return jnp.where(lane_half == (h % 2), pair, jnp.zeros_like(pair))

    def group_sum(x):
        return jnp.sum(x.reshape(G, SUBLANES, QB), axis=0)

    def group_max(x):
        return jnp.max(x.reshape(G, SUBLANES, QB), axis=0)

    def tile_rows(kt):
        return pl.ds(pl.multiple_of(kt * QB, QB), QB)

    w_t = iw_ref[...].T
    iq_ops = [head_operand(iq_ref, h) for h in range(IDX_HEADS)]

    def score_tile(kt, carry):
        kk = ikk_ref[tile_rows(kt), :]
        acc = jnp.zeros((QB, QB), f32)
        for h in range(IDX_HEADS):
            acc = acc + jnp.maximum(_dot_nt(kk, iq_ops[h]), 0.0) * w_t[h:h + 1, :]
        key = _sortable(acc * idx_scale)
        visible = jnp.logical_or(kt < qi, tri)
        keys_ref[tile_rows(kt), :] = jnp.where(visible, key, INT_MIN)
        return carry

    lax.fori_loop(0, n_tiles, score_tile, 0)

    def count(pred):
        def body(kt, cnt):
            return cnt + group_sum(jnp.where(pred(keys_ref[tile_rows(kt), :], kt), 1, 0))
        cnt = lax.fori_loop(0, n_tiles, body, jnp.zeros((SUBLANES, QB), jnp.int32))
        return jnp.sum(cnt, axis=0, keepdims=True)

    def count_ge(trial):
        return count(lambda blk, kt: blk >= trial)

    zero_row = jnp.zeros((1, QB), jnp.int32)
    ans = jnp.where(count_ge(zero_row) >= topk, zero_row, zero_row + INT_MIN)

    def bit_step(i, ans):
        trial = ans | (jnp.int32(1) << (30 - i))
        return jnp.where(count_ge(trial) >= topk, trial, ans)

    ans = lax.fori_loop(0, 31, bit_step, ans)
    thr = jnp.maximum(ans, INT_MIN + 1)

    c_ge = count_ge(thr)
    tied = jnp.logical_and(c_ge > topk, ans > INT_MIN)
    any_tied = jnp.max(jnp.where(tied, 1, 0)) > 0

    def pos_of(kt):
        return key_j + kt * QB

    def tie_cutoff():
        need = topk - count(lambda blk, kt: blk > thr)

        def pbit(i, cut):
            trial = cut | (jnp.int32(1) << (pos_bits - 1 - i))
            c = count(lambda blk, kt: jnp.logical_and(blk == thr, pos_of(kt) < trial))
            return jnp.where(c < need, trial, cut)

        cut = lax.fori_loop(0, pos_bits, pbit, zero_row)
        return jnp.where(tied, cut, jnp.int32(2 ** 30))

    cut = lax.cond(any_tied, tie_cutoff, lambda: zero_row + jnp.int32(2 ** 30))

    def mask_tile(kt, carry):
        blk = keys_ref[tile_rows(kt), :]
        sel = jnp.logical_or(blk > thr, jnp.logical_and(blk == thr, pos_of(kt) <= cut))
        madd_ref[tile_rows(kt), :] = jnp.where(sel, 0.0, NEG_BIG).astype(f32)
        return carry

    lax.fori_loop(0, n_tiles, mask_tile, 0)

    q_ops = [head_operand(q_ref, h) for h in range(DSA_HEADS)]

    def logits(kt, h):
        kp = k_ref[tile_rows(kt), (h // 2) * LANES:(h // 2 + 1) * LANES]
        rel = jnp.minimum(qi - kt, 2)
        return _dot_nt(kp, q_ops[h]) + madd_ref[tile_rows(kt), :] + bias_ref[rel * DSA_HEADS + h]

    def max_tile(kt, ms):
        return tuple(jnp.maximum(ms[h], group_max(logits(kt, h))) for h in range(DSA_HEADS))

    ms = lax.fori_loop(0, n_tiles, max_tile,
                       tuple(jnp.full((SUBLANES, QB), NEG_BIG, f32) for _ in range(DSA_HEADS)))
    ms = [jnp.max(m, axis=0, keepdims=True) for m in ms]

    acc_ref[...] = jnp.zeros_like(acc_ref)

    def pv_tile(kt, ss):
        out = []
        for h in range(DSA_HEADS):
            p = jnp.exp(logits(kt, h) - ms[h])
            out.append(ss[h] + group_sum(p))
            vt = vt_ref[h * DSA_DH:(h + 1) * DSA_DH, tile_rows(kt)]
            acc_ref[h * DSA_DH:(h + 1) * DSA_DH, :] += _dot(vt, p.astype(bf16))
        return tuple(out)

    ss = lax.fori_loop(0, n_tiles, pv_tile,
                       tuple(jnp.zeros((SUBLANES, QB), f32) for _ in range(DSA_HEADS)))
    for h in range(DSA_HEADS):
        inv = 1.0 / jnp.sum(ss[h], axis=0, keepdims=True)
        acc_ref[h * DSA_DH:(h + 1) * DSA_DH, :] = acc_ref[h * DSA_DH:(h + 1) * DSA_DH, :] * inv
    o_ref[...] = acc_ref[...].T.astype(o_ref.dtype)


def _dsa(dqk, vt, iq, ikk, iw, bias_tiles, B, S):
    T = dqk.shape[0]
    QB = DSA_QB
    nq = S // QB
    topk = min(TOPK_MAX, S // 4)
    pos_bits = max(1, int(math.ceil(math.log2(S))))
    kern = functools.partial(_dsa_kernel, topk, pos_bits)
    return pl.pallas_call(
        kern,
        grid=(B, nq),
        in_specs=[
            pl.BlockSpec((QB, DSA_W), lambda b, i: (b * nq + i, 0)),
            pl.BlockSpec((S, DSA_W), lambda b, i: (b, 1)),
            pl.BlockSpec((None, DSA_W, S), lambda b, i: (b, 0, 0)),
            pl.BlockSpec((QB, IDX_Q), lambda b, i: (b * nq + i, 0)),
            pl.BlockSpec((S, LANES), lambda b, i: (b, 0)),
            pl.BlockSpec((QB, LANES), lambda b, i: (b * nq + i, 0)),
            _const_spec(bias_tiles.shape),
        ],
        out_specs=pl.BlockSpec((QB, DSA_W), lambda b, i: (b * nq + i, 0)),
        out_shape=jax.ShapeDtypeStruct((T, DSA_W), bf16),
        scratch_shapes=[pltpu.VMEM((S, QB), jnp.int32),
                        pltpu.VMEM((S, QB), f32),
                        pltpu.VMEM((DSA_W, QB), f32)],
        compiler_params=pltpu.CompilerParams(dimension_semantics=("arbitrary", "arbitrary"),
                                             vmem_limit_bytes=VMEM_LIMIT_BYTES),
        name="dsa",
    )(dqk, dqk, vt, iq, ikk, iw, bias_tiles)


def _rel_bucket(dist):
    max_exact = REL_BUCKETS // 2
    d = jnp.maximum(dist, 1).astype(f32)
    large = max_exact + (jnp.log(d / max_exact) / math.log(REL_MAX_DIST / max_exact)
                         * (REL_BUCKETS - max_exact)).astype(jnp.int32)
    large = jnp.minimum(large, REL_BUCKETS - 1)
    return jnp.where(dist < max_exact, dist, large)


def _bias_tiles(rel_bias):
    QB = DSA_QB
    j = jnp.arange(QB, dtype=jnp.int32)[:, None]
    i = jnp.arange(QB, dtype=jnp.int32)[None, :]
    far = rel_bias[REL_BUCKETS - 1].astype(f32)
    tiles = []
    for rel in range(3):
        dist = jnp.maximum(rel * QB + i - j, 0)
        b = rel_bias[_rel_bucket(dist)].astype(f32) - far
        tiles.append(jnp.transpose(b, (2, 0, 1)))
    return jnp.concatenate(tiles, axis=0)


def _mix_kernel(h_ref, oa_ref, ob_ref, gate_ref, wa_ref, wb_ref, wo_ref, gain_ref, out_ref):
    D = h_ref.shape[1]
    ya = _dot(oa_ref[...], wa_ref[...])
    yb = _dot(ob_ref[...], wb_ref[...])
    ga = jax.nn.sigmoid(gate_ref[:, 0:D].astype(f32))
    gb = jax.nn.sigmoid(gate_ref[:, D:2 * D].astype(f32))
    mixed = (ga * ya + gb * yb).astype(bf16)
    m = _dot(mixed, wo_ref[...])
    out_ref[...] = h_ref[...] + _rms(m, gain_ref[...])


def _mix(h2d, oa, ob, gates, wa, wb, wo, gain):
    T, D = h2d.shape
    tm = min(TOKEN_TILE, T)
    tok = lambda n: pl.BlockSpec((tm, n), lambda t: (t, 0))
    return pl.pallas_call(
        _mix_kernel,
        grid=(T // tm,),
        in_specs=[tok(D), tok(GLA_V), tok(DSA_W), tok(2 * D),
                  _const_spec(wa.shape), _const_spec(wb.shape), _const_spec(wo.shape),
                  _const_spec((1, D))],
        out_specs=tok(D),
        out_shape=jax.ShapeDtypeStruct((T, D), f32),
        compiler_params=pltpu.CompilerParams(dimension_semantics=("arbitrary",),
                                             vmem_limit_bytes=VMEM_LIMIT_BYTES),
        name="mix",
    )(h2d, oa, ob, gates, wa, wb, wo, gain)


def _mlp_kernel(h_ref, p_ref, w1_ref, w2_ref, wple_ref, wg_ref, g_pre_ref, g_post_ref, g_ple_ref,
                out_ref, acc_ref):
    h1 = h_ref[...]
    um = _rms(h1, g_pre_ref[...]).astype(bf16)
    d_ff = w1_ref.shape[1]
    for c in range(d_ff // FF_CHUNK):
        cs = slice(c * FF_CHUNK, (c + 1) * FF_CHUNK)
        hid = jnp.maximum(_dot(um, w1_ref[:, cs]), 0.0)
        part = _dot((hid * hid).astype(bf16), w2_ref[cs, :])
        if c == 0:
            acc_ref[...] = part
        else:
            acc_ref[...] += part
    h2 = h1 + _rms(acc_ref[...], g_post_ref[...])
    e = _dot(p_ref[...].astype(bf16), wple_ref[...]) * jax.nn.sigmoid(_dot(h2.astype(bf16), wg_ref[...]))
    out_ref[...] = h2 + _rms(e, g_ple_ref[...])


def _mlp(h2d, p2d, w1, w2, wple, wg, g_pre, g_post, g_ple):
    T, D = h2d.shape
    tm = min(TOKEN_TILE, T)
    tok = lambda n: pl.BlockSpec((tm, n), lambda t: (t, 0))
    return pl.pallas_call(
        _mlp_kernel,
        grid=(T // tm,),
        in_specs=[tok(D), tok(p2d.shape[1]),
                  _const_spec(w1.shape), _const_spec(w2.shape), _const_spec(wple.shape),
                  _const_spec(wg.shape), _const_spec((1, D)), _const_spec((1, D)), _const_spec((1, D))],
        out_specs=tok(D),
        out_shape=jax.ShapeDtypeStruct((T, D), f32),
        scratch_shapes=[pltpu.VMEM((tm, D), f32)],
        compiler_params=pltpu.CompilerParams(dimension_semantics=("arbitrary",),
                                             vmem_limit_bytes=VMEM_LIMIT_BYTES),
        name="mlp",
    )(h2d, p2d, w1, w2, wple, wg, g_pre, g_post, g_ple)


def _regroup_w_in(w, d_model):
    sizes = (GLA_QK, GLA_QK, GLA_V, GLA_V, GLA_GATE_RANK, DSA_W, DSA_W, DSA_W,
             IDX_Q, IDX_DIM, IDX_HEADS, d_model, d_model)
    pts = np.cumsum(sizes)[:-1].tolist()
    gq, gk, gv, gg, glr, dq, dk, dv, iq, ik, iw, ga, gb = jnp.split(w, pts, axis=1)
    pad = lambda a: jnp.pad(a, ((0, 0), (0, LANES - a.shape[1])))
    w_all = jnp.concatenate(
        [gq * (GLA_DK ** -0.5), gk, gv, gg, dq * (DSA_DH ** -0.5), dk, iq, ik, ik, ga, gb,
         pad(glr), pad(iw)], axis=1).astype(bf16)
    assert w_all.shape[1] == _P_TOTAL
    return w_all, dv.T.astype(bf16)


def kernel(x, p, rel_bias, ln_mix_pre, ln_mix_post, ln_mlp_pre, ln_mlp_post, ln_ple_post, w_in,
           gla_gate_w2, gla_gate_b, gla_norm, w_branch_a, w_branch_b, w_out, w_mlp_in, w_mlp_out,
           w_ple, w_ple_gate):
    B, S, D = x.shape
    depth = w_in.shape[0]
    T = B * S
    assert S % DSA_QB == 0 and S % GLA_STEP == 0 and T % TOKEN_TILE == 0
    bias_tiles = _bias_tiles(rel_bias)
    h = x.reshape(T, D)
    row = lambda a: a.reshape(1, -1).astype(f32)
    for i in range(depth):
        w_all, w_vt = _regroup_w_in(w_in[i], D)
        w2p = jnp.pad(gla_gate_w2[i], ((0, LANES - GLA_GATE_RANK), (0, 0))).astype(bf16)
        gla_in, glog, dqk, vt, iq, ikk, iw, gates = _proj(
            h, row(ln_mix_pre[i]), w_all, w_vt, w2p, row(gla_gate_b[i]), B, S)
        o_a = _gla(gla_in, glog, row(gla_norm[i]), B, S)
        o_b = _dsa(dqk, vt, iq, ikk, iw, bias_tiles, B, S)
        h = _mix(h, o_a, o_b, gates, w_branch_a[i].astype(bf16), w_branch_b[i].astype(bf16),
                 w_out[i].astype(bf16), row(ln_mix_post[i]))
        h = _mlp(h, p[i].reshape(T, -1), w_mlp_in[i].astype(bf16), w_mlp_out[i].astype(bf16),
                 w_ple[i].astype(bf16), w_ple_gate[i].astype(bf16),
                 row(ln_mlp_pre[i]), row(ln_mlp_post[i]), row(ln_ple_post[i]))
    return h.reshape(B, S, D)
```

```python
import functools
import math

import numpy as np
import jax
import jax.numpy as jnp
from jax import lax
from jax.experimental import pallas as pl
from jax.experimental.pallas import tpu as pltpu

GLA_HEADS = 4
GLA_DK = 64
GLA_DV = 128
GLA_GATE_RANK = 16
GLA_GATE_NORM = 16.0
GLA_CHUNK = 64
DSA_HEADS = 8
DSA_DH = 64
IDX_HEADS = 8
IDX_DIM = 64
TOPK_MAX = 256
REL_BUCKETS = 32
REL_MAX_DIST = 128
EPS = 1e-6

GLA_QK = GLA_HEADS * GLA_DK
GLA_V = GLA_HEADS * GLA_DV
DSA_W = DSA_HEADS * DSA_DH
IDX_Q = IDX_HEADS * IDX_DIM

LANES = 128
SUBLANES = 8
VMEM_LIMIT_BYTES = 56 * 1024 * 1024

TOKEN_TILE = 512
GLA_STEP = 256
DSA_QB = 256
FF_CHUNK = 512

INT_MIN = np.int32(-2 ** 31)
NEG_BIG = -1e30

bf16 = jnp.bfloat16
f32 = jnp.float32


def _const_spec(shape):
    nd = len(shape)
    return pl.BlockSpec(shape, lambda *_: (0,) * nd, pipeline_mode=pl.Buffered(1))


def _rms(xf, gain):
    return xf * lax.rsqrt(jnp.mean(xf * xf, axis=-1, keepdims=True) + EPS) * gain


def _dot(a, b):
    return jnp.dot(a, b, preferred_element_type=f32)


def _dot_nt(a, b):
    return lax.dot_general(a, b, (((1,), (1,)), ((), ())), preferred_element_type=f32)


def _dot_tn(a, b):
    return lax.dot_general(a, b, (((0,), (0,)), ((), ())), preferred_element_type=f32)


_P_GLA = (0, 1536)
_P_DQK = (1536, 2560)
_P_IQ = (2560, 3072)
_P_IKK = (3072, 3200)
_P_GATE = (3200, 5248)
_P_GLR = (5248, 5376)
_P_IW = (5376, 5504)
_P_TOTAL = 5504


def _proj_kernel(h_ref, gain_ref, w_ref, wvt_ref, w2_ref, gb_ref,
                 gla_ref, glog_ref, dqk_ref, vt_ref, iq_ref, ikk_ref, iw_ref, gate_ref):
    u = _rms(h_ref[...], gain_ref[...]).astype(bf16)

    def proj(span):
        return _dot(u, w_ref[:, span[0]:span[1]])

    gla_ref[...] = proj(_P_GLA).astype(bf16)
    dqk_ref[...] = proj(_P_DQK).astype(bf16)
    iq_ref[...] = proj(_P_IQ).astype(bf16)
    ikk_ref[...] = proj(_P_IKK).astype(bf16)
    gate_ref[...] = proj(_P_GATE).astype(bf16)
    iw_ref[...] = proj(_P_IW)
    vt_ref[...] = _dot_nt(wvt_ref[...], u).astype(bf16)
    glr = proj(_P_GLR).astype(bf16)
    x = _dot(glr, w2_ref[...]) + gb_ref[...]
    ls = jnp.minimum(x, 0.0) - jnp.log(1.0 + jnp.exp(-jnp.abs(x)))
    glog_ref[...] = ls * (1.0 / GLA_GATE_NORM)


def _proj(h2d, gain, w_all, w_vt, w2p, gb, B, S):
    T, D = h2d.shape
    tm = min(TOKEN_TILE, S)
    nS = S // tm
    tok = lambda n: pl.BlockSpec((tm, n), lambda t: (t, 0))
    out_shape = (
        jax.ShapeDtypeStruct((T, 1536), bf16),
        jax.ShapeDtypeStruct((T, GLA_QK), f32),
        jax.ShapeDtypeStruct((T, 2 * DSA_W), bf16),
        jax.ShapeDtypeStruct((B, DSA_W, S), bf16),
        jax.ShapeDtypeStruct((T, IDX_Q), bf16),
        jax.ShapeDtypeStruct((T, LANES), bf16),
        jax.ShapeDtypeStruct((T, LANES), f32),
        jax.ShapeDtypeStruct((T, 2 * D), bf16),
    )
    out_specs = (
        tok(1536), tok(GLA_QK), tok(2 * DSA_W),
        pl.BlockSpec((None, DSA_W, tm), lambda t: (t // nS, 0, t % nS)),
        tok(IDX_Q), tok(LANES), tok(LANES), tok(2 * D),
    )
    return pl.pallas_call(
        _proj_kernel,
        grid=(T // tm,),
        in_specs=[tok(D), _const_spec((1, D)), _const_spec(w_all.shape), _const_spec(w_vt.shape),
                  _const_spec(w2p.shape), _const_spec((1, GLA_QK))],
        out_specs=out_specs,
        out_shape=out_shape,
        compiler_params=pltpu.CompilerParams(dimension_semantics=("arbitrary",),
                                             vmem_limit_bytes=VMEM_LIMIT_BYTES),
        name="proj",
    )(h2d, gain, w_all, w_vt, w2p, gb)


def _gla_kernel(x_ref, glog_ref, gn_ref, o_ref, state_ref):
    @pl.when(pl.program_id(1) == 0)
    def _():
        state_ref[...] = jnp.zeros_like(state_ref)

    C = GLA_CHUNK
    row = lax.broadcasted_iota(jnp.int32, (C, C), 0)
    col = lax.broadcasted_iota(jnp.int32, (C, C), 1)
    causal = col <= row
    tril = causal.astype(f32)
    gn = gn_ref[...]

    for c in range(GLA_STEP // C):
        rows = slice(c * C, (c + 1) * C)
        q = x_ref[rows, 0:GLA_QK].astype(f32)
        k = x_ref[rows, GLA_QK:2 * GLA_QK].astype(f32)
        g = glog_ref[rows, :]
        b = jnp.dot(tril, g, preferred_element_type=f32, precision=lax.Precision.HIGHEST)
        b_last = b[C - 1:C, :]
        anchor = 0.5 * b_last
        q_inter = (q * jnp.exp(b)).astype(bf16)
        q_s = (q * jnp.exp(b - anchor)).astype(bf16)
        k_s = (k * jnp.exp(anchor - b)).astype(bf16)
        k_d = (k * jnp.exp(b_last - b)).astype(bf16)
        decay = jnp.exp(b_last)
        state = state_ref[...]
        state_b = state.astype(bf16)
        new_cols = []
        for h in range(GLA_HEADS):
            hs = slice(h * GLA_DK, (h + 1) * GLA_DK)
            vs = slice(2 * GLA_QK + h * GLA_DV, 2 * GLA_QK + (h + 1) * GLA_DV)
            gs = slice(2 * GLA_QK + GLA_V + h * GLA_DV, 2 * GLA_QK + GLA_V + (h + 1) * GLA_DV)
            v = x_ref[rows, vs]
            attn = jnp.where(causal, _dot_nt(q_s[:, hs], k_s[:, hs]), 0.0).astype(bf16)
            o = _dot(attn, v) + _dot_nt(q_inter[:, hs], state_b[:, hs])
            new_cols.append(_dot_tn(v, k_d[:, hs]))
            gg = x_ref[rows, gs].astype(f32)
            o_ref[rows, h * GLA_DV:(h + 1) * GLA_DV] = (
                _rms(o, gn) * (gg * jax.nn.sigmoid(gg))).astype(o_ref.dtype)
        state_ref[...] = state * decay + jnp.concatenate(new_cols, axis=1)


def _gla(gla_in, glog, gn, B, S):
    T = gla_in.shape[0]
    step = GLA_STEP
    nS = S // step
    return pl.pallas_call(
        _gla_kernel,
        grid=(B, nS),
        in_specs=[pl.BlockSpec((step, 1536), lambda b, s: (b * nS + s, 0)),
                  pl.BlockSpec((step, GLA_QK), lambda b, s: (b * nS + s, 0)),
                  _const_spec((1, GLA_DV))],
        out_specs=pl.BlockSpec((step, GLA_V), lambda b, s: (b * nS + s, 0)),
        out_shape=jax.ShapeDtypeStruct((T, GLA_V), bf16),
        scratch_shapes=[pltpu.VMEM((GLA_DV, GLA_QK), f32)],
        compiler_params=pltpu.CompilerParams(dimension_semantics=("arbitrary", "arbitrary"),
                                             vmem_limit_bytes=VMEM_LIMIT_BYTES),
        name="gla",
    )(gla_in, glog, gn)


def _sortable(x):
    bits = lax.bitcast_convert_type(x + 0.0, jnp.int32)
    return bits ^ ((bits >> 31) & jnp.int32(0x7FFFFFFF))


def _dsa_kernel(topk, pos_bits, q_ref, k_ref, vt_ref, iq_ref, ikk_ref, iw_ref, bias_ref,
                o_ref, keys_ref, madd_ref, acc_ref, wiq_ref, wq_ref, p_ref, mt_ref, st_ref):
    QB = DSA_QB
    G = QB // SUBLANES
    qi = pl.program_id(1)
    n_tiles = qi + 1
    idx_scale = (IDX_HEADS ** -0.5) * (IDX_DIM ** -0.5)

    row_half = lax.broadcasted_iota(jnp.int32, (LANES, QB), 0) // DSA_DH
    key_j = lax.broadcasted_iota(jnp.int32, (QB, QB), 0)
    qry_i = lax.broadcasted_iota(jnp.int32, (QB, QB), 1)
    tri = key_j <= qry_i

    def stage_head_operands(src_ref, dst_ref):
        t = src_ref[...].astype(f32).T
        for h in range(DSA_HEADS):
            pair = t[(h // 2) * LANES:(h // 2 + 1) * LANES, :]
            dst_ref[h] = jnp.where(row_half == (h % 2), pair, 0.0).astype(bf16)

    def group_sum(x):
        return jnp.sum(x.reshape(G, SUBLANES, QB), axis=0)

    def group_max(x):
        return jnp.max(x.reshape(G, SUBLANES, QB), axis=0)

    def tile_rows(kt):
        return pl.ds(pl.multiple_of(kt * QB, QB), QB)

    w_t = iw_ref[...].T
    stage_head_operands(iq_ref, wiq_ref)

    def score_tile(kt, carry):
        kk = ikk_ref[tile_rows(kt), :]
        acc = jnp.zeros((QB, QB), f32)
        for h in range(IDX_HEADS):
            acc = acc + jnp.maximum(_dot(kk, wiq_ref[h]), 0.0) * w_t[h:h + 1, :]
        key = _sortable(acc * idx_scale)
        visible = jnp.logical_or(kt < qi, tri)
        keys_ref[tile_rows(kt), :] = jnp.where(visible, key, INT_MIN)
        return carry

    lax.fori_loop(0, n_tiles, score_tile, 0)

    def count(pred):
        def body(kt, cnt):
            return cnt + group_sum(jnp.where(pred(keys_ref[tile_rows(kt), :], kt), 1, 0))
        cnt = lax.fori_loop(0, n_tiles, body, jnp.zeros((SUBLANES, QB), jnp.int32))
        return jnp.sum(cnt, axis=0, keepdims=True)

    def count_ge(trial):
        return count(lambda blk, kt: blk >= trial)

    zero_row = jnp.zeros((1, QB), jnp.int32)
    ans = jnp.where(count_ge(zero_row) >= topk, zero_row, zero_row + INT_MIN)

    def bit_step(i, ans):
        trial = ans | (jnp.int32(1) << (30 - i))
        return jnp.where(count_ge(trial) >= topk, trial, ans)

    ans = lax.fori_loop(0, 31, bit_step, ans)
    thr = jnp.maximum(ans, INT_MIN + 1)

    c_ge = count_ge(thr)
    tied = jnp.logical_and(c_ge > topk, ans > INT_MIN)
    any_tied = jnp.max(jnp.where(tied, 1, 0)) > 0

    def pos_of(kt):
        return key_j + kt * QB

    def tie_cutoff():
        need = topk - count(lambda blk, kt: blk > thr)

        def pbit(i, cut):
            trial = cut | (jnp.int32(1) << (pos_bits - 1 - i))
            c = count(lambda blk, kt: jnp.logical_and(blk == thr, pos_of(kt) < trial))
            return jnp.where(c < need, trial, cut)

        cut = lax.fori_loop(0, pos_bits, pbit, zero_row)
        return jnp.where(tied, cut, jnp.int32(2 ** 30))

    cut = lax.cond(any_tied, tie_cutoff, lambda: zero_row + jnp.int32(2 ** 30))

    def mask_tile(kt, carry):
        blk = keys_ref[tile_rows(kt), :]
        sel = jnp.logical_or(blk > thr, jnp.logical_and(blk == thr, pos_of(kt) <= cut))
        madd_ref[tile_rows(kt), :] = jnp.where(sel, 0.0, NEG_BIG).astype(f32)
        return carry

    lax.fori_loop(0, n_tiles, mask_tile, 0)

    stage_head_operands(q_ref, wq_ref)

    def prob_tile(kt, m_run):
        rel = jnp.minimum(qi - kt, 2)
        out = []
        for h in range(DSA_HEADS):
            kp = k_ref[tile_rows(kt), (h // 2) * LANES:(h // 2 + 1) * LANES]
            l = _dot(kp, wq_ref[h]) + madd_ref[tile_rows(kt), :] + bias_ref[rel * DSA_HEADS + h]
            m_t = jnp.max(group_max(l), axis=0, keepdims=True)
            p = jnp.exp(l - m_t)
            p_ref[h, tile_rows(kt), :] = p.astype(bf16)
            mt_ref[kt * DSA_HEADS + h] = m_t
            st_ref[kt * DSA_HEADS + h] = jnp.sum(group_sum(p), axis=0, keepdims=True)
            out.append(jnp.maximum(m_run[h], m_t))
        return tuple(out)

    m_all = lax.fori_loop(0, n_tiles, prob_tile,
                          tuple(jnp.full((1, QB), NEG_BIG, f32) for _ in range(DSA_HEADS)))

    acc_ref[...] = jnp.zeros_like(acc_ref)

    def pv_tile(kt, ss):
        out = []
        for h in range(DSA_HEADS):
            w = jnp.exp(mt_ref[kt * DSA_HEADS + h] - m_all[h])
            out.append(ss[h] + w * st_ref[kt * DSA_HEADS + h])
            vt = vt_ref[h * DSA_DH:(h + 1) * DSA_DH, tile_rows(kt)]
            acc_ref[h * DSA_DH:(h + 1) * DSA_DH, :] += _dot(vt, p_ref[h, tile_rows(kt), :]) * w
        return tuple(out)

    ss = lax.fori_loop(0, n_tiles, pv_tile,
                       tuple(jnp.zeros((1, QB), f32) for _ in range(DSA_HEADS)))
    for h in range(DSA_HEADS):
        acc_ref[h * DSA_DH:(h + 1) * DSA_DH, :] = acc_ref[h * DSA_DH:(h + 1) * DSA_DH, :] * (1.0 / ss[h])
    o_ref[...] = acc_ref[...].T.astype(o_ref.dtype)


def _dsa(dqk, vt, iq, ikk, iw, bias_tiles, B, S):
    T = dqk.shape[0]
    QB = DSA_QB
    nq = S // QB
    topk = min(TOPK_MAX, S // 4)
    pos_bits = max(1, int(math.ceil(math.log2(S))))
    kern = functools.partial(_dsa_kernel, topk, pos_bits)
    return pl.pallas_call(
        kern,
        grid=(B, nq),
        in_specs=[
            pl.BlockSpec((QB, DSA_W), lambda b, i: (b * nq + i, 0)),
            pl.BlockSpec((S, DSA_W), lambda b, i: (b, 1), pipeline_mode=pl.Buffered(1)),
            pl.BlockSpec((None, DSA_W, S), lambda b, i: (b, 0, 0), pipeline_mode=pl.Buffered(1)),
            pl.BlockSpec((QB, IDX_Q), lambda b, i: (b * nq + i, 0)),
            pl.BlockSpec((S, LANES), lambda b, i: (b, 0), pipeline_mode=pl.Buffered(1)),
            pl.BlockSpec((QB, LANES), lambda b, i: (b * nq + i, 0)),
            _const_spec(bias_tiles.shape),
        ],
        out_specs=pl.BlockSpec((QB, DSA_W), lambda b, i: (b * nq + i, 0)),
        out_shape=jax.ShapeDtypeStruct((T, DSA_W), bf16),
        scratch_shapes=[pltpu.VMEM((S, QB), jnp.int32),
                        pltpu.VMEM((S, QB), f32),
                        pltpu.VMEM((DSA_W, QB), f32),
                        pltpu.VMEM((IDX_HEADS, LANES, QB), bf16),
                        pltpu.VMEM((DSA_HEADS, LANES, QB), bf16),
                        pltpu.VMEM((DSA_HEADS, S, QB), bf16),
                        pltpu.VMEM((nq * DSA_HEADS, 1, QB), f32),
                        pltpu.VMEM((nq * DSA_HEADS, 1, QB), f32)],
        compiler_params=pltpu.CompilerParams(dimension_semantics=("arbitrary", "arbitrary"),
                                             vmem_limit_bytes=VMEM_LIMIT_BYTES),
        name="dsa",
    )(dqk, dqk, vt, iq, ikk, iw, bias_tiles)


def _rel_bucket(dist):
    max_exact = REL_BUCKETS // 2
    d = jnp.maximum(dist, 1).astype(f32)
    large = max_exact + (jnp.log(d / max_exact) / math.log(REL_MAX_DIST / max_exact)
                         * (REL_BUCKETS - max_exact)).astype(jnp.int32)
    large = jnp.minimum(large, REL_BUCKETS - 1)
    return jnp.where(dist < max_exact, dist, large)


def _bias_tiles(rel_bias):
    QB = DSA_QB
    j = jnp.arange(QB, dtype=jnp.int32)[:, None]
    i = jnp.arange(QB, dtype=jnp.int32)[None, :]
    far = rel_bias[REL_BUCKETS - 1].astype(f32)
    tiles = []
    for rel in range(3):
        dist = jnp.maximum(rel * QB + i - j, 0)
        onehot = (_rel_bucket(dist)[None] == jnp.arange(REL_BUCKETS, dtype=jnp.int32)[:, None, None])
        tiles.append(jnp.einsum('bh,bji->hji', rel_bias.astype(f32) - far[None, :], onehot.astype(f32),
                                precision=lax.Precision.HIGHEST))
    return jnp.concatenate(tiles, axis=0)


def _mix_kernel(h_ref, oa_ref, ob_ref, gate_ref, wa_ref, wb_ref, wo_ref, gain_ref, out_ref):
    D = h_ref.shape[1]
    ya = _dot(oa_ref[...], wa_ref[...])
    yb = _dot(ob_ref[...], wb_ref[...])
    ga = jax.nn.sigmoid(gate_ref[:, 0:D].astype(f32))
    gb = jax.nn.sigmoid(gate_ref[:, D:2 * D].astype(f32))
    mixed = (ga * ya + gb * yb).astype(bf16)
    m = _dot(mixed, wo_ref[...])
    out_ref[...] = h_ref[...] + _rms(m, gain_ref[...])


def _mix(h2d, oa, ob, gates, wa, wb, wo, gain):
    T, D = h2d.shape
    tm = min(TOKEN_TILE, T)
    tok = lambda n: pl.BlockSpec((tm, n), lambda t: (t, 0))
    return pl.pallas_call(
        _mix_kernel,
        grid=(T // tm,),
        in_specs=[tok(D), tok(GLA_V), tok(DSA_W), tok(2 * D),
                  _const_spec(wa.shape), _const_spec(wb.shape), _const_spec(wo.shape),
                  _const_spec((1, D))],
        out_specs=tok(D),
        out_shape=jax.ShapeDtypeStruct((T, D), f32),
        compiler_params=pltpu.CompilerParams(dimension_semantics=("arbitrary",),
                                             vmem_limit_bytes=VMEM_LIMIT_BYTES),
        name="mix",
    )(h2d, oa, ob, gates, wa, wb, wo, gain)


def _mlp_kernel(h_ref, p_ref, w1_ref, w2_ref, wple_ref, wg_ref, g_pre_ref, g_post_ref, g_ple_ref,
                out_ref, acc_ref):
    h1 = h_ref[...]
    um = _rms(h1, g_pre_ref[...]).astype(bf16)
    d_ff = w1_ref.shape[1]
    for c in range(d_ff // FF_CHUNK):
        cs = slice(c * FF_CHUNK, (c + 1) * FF_CHUNK)
        hid = jnp.maximum(_dot(um, w1_ref[:, cs]), 0.0)
        part = _dot((hid * hid).astype(bf16), w2_ref[cs, :])
        if c == 0:
            acc_ref[...] = part
        else:
            acc_ref[...] += part
    h2 = h1 + _rms(acc_ref[...], g_post_ref[...])
    e = _dot(p_ref[...].astype(bf16), wple_ref[...]) * jax.nn.sigmoid(_dot(h2.astype(bf16), wg_ref[...]))
    out_ref[...] = h2 + _rms(e, g_ple_ref[...])


def _mlp(h2d, p2d, w1, w2, wple, wg, g_pre, g_post, g_ple):
    T, D = h2d.shape
    tm = min(TOKEN_TILE, T)
    tok = lambda n: pl.BlockSpec((tm, n), lambda t: (t, 0))
    return pl.pallas_call(
        _mlp_kernel,
        grid=(T // tm,),
        in_specs=[tok(D), tok(p2d.shape[1]),
                  _const_spec(w1.shape), _const_spec(w2.shape), _const_spec(wple.shape),
                  _const_spec(wg.shape), _const_spec((1, D)), _const_spec((1, D)), _const_spec((1, D))],
        out_specs=tok(D),
        out_shape=jax.ShapeDtypeStruct((T, D), f32),
        scratch_shapes=[pltpu.VMEM((tm, D), f32)],
        compiler_params=pltpu.CompilerParams(dimension_semantics=("arbitrary",),
                                             vmem_limit_bytes=VMEM_LIMIT_BYTES),
        name="mlp",
    )(h2d, p2d, w1, w2, wple, wg, g_pre, g_post, g_ple)


def _regroup_w_in(w, d_model):
    sizes = (GLA_QK, GLA_QK, GLA_V, GLA_V, GLA_GATE_RANK, DSA_W, DSA_W, DSA_W,
             IDX_Q, IDX_DIM, IDX_HEADS, d_model, d_model)
    pts = np.cumsum(sizes)[:-1].tolist()
    gq, gk, gv, gg, glr, dq, dk, dv, iq, ik, iw, ga, gb = jnp.split(w, pts, axis=1)
    pad = lambda a: jnp.pad(a, ((0, 0), (0, LANES - a.shape[1])))
    w_all = jnp.concatenate(
        [gq * (GLA_DK ** -0.5), gk, gv, gg, dq * (DSA_DH ** -0.5), dk, iq, ik, ik, ga, gb,
         pad(glr), pad(iw)], axis=1).astype(bf16)
    assert w_all.shape[1] == _P_TOTAL
    return w_all, dv.T.astype(bf16)


def kernel(x, p, rel_bias, ln_mix_pre, ln_mix_post, ln_mlp_pre, ln_mlp_post, ln_ple_post, w_in,
           gla_gate_w2, gla_gate_b, gla_norm, w_branch_a, w_branch_b, w_out, w_mlp_in, w_mlp_out,
           w_ple, w_ple_gate):
    B, S, D = x.shape
    depth = w_in.shape[0]
    T = B * S
    assert S % DSA_QB == 0 and S % GLA_STEP == 0 and T % TOKEN_TILE == 0
    bias_tiles = _bias_tiles(rel_bias)
    h = x.reshape(T, D)
    row = lambda a: a.reshape(1, -1).astype(f32)
    for i in range(depth):
        w_all, w_vt = _regroup_w_in(w_in[i], D)
        w2p = jnp.pad(gla_gate_w2[i], ((0, LANES - GLA_GATE_RANK), (0, 0))).astype(bf16)
        gla_in, glog, dqk, vt, iq, ikk, iw, gates = _proj(
            h, row(ln_mix_pre[i]), w_all, w_vt, w2p, row(gla_gate_b[i]), B, S)
        o_a = _gla(gla_in, glog, row(gla_norm[i]), B, S)
        o_b = _dsa(dqk, vt, iq, ikk, iw, bias_tiles, B, S)
        h = _mix(h, o_a, o_b, gates, w_branch_a[i].astype(bf16), w_branch_b[i].astype(bf16),
                 w_out[i].astype(bf16), row(ln_mix_post[i]))
        h = _mlp(h, p[i].reshape(T, -1), w_mlp_in[i].astype(bf16), w_mlp_out[i].astype(bf16),
                 w_ple[i].astype(bf16), w_ple_gate[i].astype(bf16),
                 row(ln_mlp_pre[i]), row(ln_mlp_post[i]), row(ln_ple_post[i]))
    return h.reshape(B, S, D)
```

```python
import functools
import math

import numpy as np
import jax
import jax.numpy as jnp
from jax import lax
from jax.experimental import pallas as pl
from jax.experimental.pallas import tpu as pltpu

GLA_HEADS = 4
GLA_DK = 64
GLA_DV = 128
GLA_GATE_RANK = 16
GLA_GATE_NORM = 16.0
GLA_CHUNK = 64
DSA_HEADS = 8
DSA_DH = 64
IDX_HEADS = 8
IDX_DIM = 64
TOPK_MAX = 256
REL_BUCKETS = 32
REL_MAX_DIST = 128
EPS = 1e-6

GLA_QK = GLA_HEADS * GLA_DK
GLA_V = GLA_HEADS * GLA_DV
DSA_W = DSA_HEADS * DSA_DH
IDX_Q = IDX_HEADS * IDX_DIM

LANES = 128
SUBLANES = 8
VMEM_LIMIT_BYTES = 56 * 1024 * 1024

TOKEN_TILE = 512
GLA_STEP = 256
DSA_QB = 256
FF_CHUNK = 512

LOG2E = math.log2(math.e)
INT_MIN = np.int32(-2 ** 31)
NEG_BIG = -1e30

bf16 = jnp.bfloat16
f32 = jnp.float32


def _const_spec(shape):
    nd = len(shape)
    return pl.BlockSpec(shape, lambda *_: (0,) * nd, pipeline_mode=pl.Buffered(1))


def _rms(xf, gain):
    return xf * lax.rsqrt(jnp.mean(xf * xf, axis=-1, keepdims=True) + EPS) * gain


def _dot(a, b):
    return jnp.dot(a, b, preferred_element_type=f32)


def _dot_nt(a, b):
    return lax.dot_general(a, b, (((1,), (1,)), ((), ())), preferred_element_type=f32)


def _dot_tn(a, b):
    return lax.dot_general(a, b, (((0,), (0,)), ((), ())), preferred_element_type=f32)


_P_GLA = (0, 1536)
_P_DQK = (1536, 2560)
_P_IQ = (2560, 3072)
_P_IKK = (3072, 3200)
_P_GATE = (3200, 5248)
_P_GLR = (5248, 5376)
_P_IW = (5376, 5504)
_P_TOTAL = 5504


def _proj_kernel(h_ref, gain_ref, w_ref, wvt_ref, w2_ref, gb_ref,
                 gla_ref, glog_ref, dqk_ref, vt_ref, iq_ref, ikk_ref, iw_ref, gate_ref):
    u = _rms(h_ref[...], gain_ref[...]).astype(bf16)

    def proj(span):
        return _dot(u, w_ref[:, span[0]:span[1]])

    gla_ref[...] = proj(_P_GLA).astype(bf16)
    dqk_ref[...] = proj(_P_DQK).astype(bf16)
    iq_ref[...] = proj(_P_IQ).astype(bf16)
    ikk_ref[...] = proj(_P_IKK).astype(bf16)
    gate_ref[...] = proj(_P_GATE).astype(bf16)
    iw_ref[...] = proj(_P_IW)
    vt_ref[...] = _dot_nt(wvt_ref[...], u).astype(bf16)
    glr = proj(_P_GLR).astype(bf16)
    x = _dot(glr, w2_ref[...]) + gb_ref[...]
    ls = jnp.minimum(x, 0.0) - jnp.log(1.0 + jnp.exp(-jnp.abs(x)))
    glog_ref[...] = ls * (1.0 / GLA_GATE_NORM)


def _proj(h2d, gain, w_all, w_vt, w2p, gb, B, S):
    T, D = h2d.shape
    tm = min(TOKEN_TILE, S)
    nS = S // tm
    tok = lambda n: pl.BlockSpec((tm, n), lambda t: (t, 0))
    out_shape = (
        jax.ShapeDtypeStruct((T, 1536), bf16),
        jax.ShapeDtypeStruct((T, GLA_QK), f32),
        jax.ShapeDtypeStruct((T, 2 * DSA_W), bf16),
        jax.ShapeDtypeStruct((B, DSA_W, S), bf16),
        jax.ShapeDtypeStruct((T, IDX_Q), bf16),
        jax.ShapeDtypeStruct((T, LANES), bf16),
        jax.ShapeDtypeStruct((T, LANES), f32),
        jax.ShapeDtypeStruct((T, 2 * D), bf16),
    )
    out_specs = (
        tok(1536), tok(GLA_QK), tok(2 * DSA_W),
        pl.BlockSpec((None, DSA_W, tm), lambda t: (t // nS, 0, t % nS)),
        tok(IDX_Q), tok(LANES), tok(LANES), tok(2 * D),
    )
    return pl.pallas_call(
        _proj_kernel,
        grid=(T // tm,),
        in_specs=[tok(D), _const_spec((1, D)), _const_spec(w_all.shape), _const_spec(w_vt.shape),
                  _const_spec(w2p.shape), _const_spec((1, GLA_QK))],
        out_specs=out_specs,
        out_shape=out_shape,
        compiler_params=pltpu.CompilerParams(dimension_semantics=("arbitrary",),
                                             vmem_limit_bytes=VMEM_LIMIT_BYTES),
        name="proj",
    )(h2d, gain, w_all, w_vt, w2p, gb)


def _gla_kernel(x_ref, glog_ref, gn_ref, o_ref, state_ref):
    @pl.when(pl.program_id(1) == 0)
    def _():
        state_ref[...] = jnp.zeros_like(state_ref)

    C = GLA_CHUNK
    row = lax.broadcasted_iota(jnp.int32, (C, C), 0)
    col = lax.broadcasted_iota(jnp.int32, (C, C), 1)
    causal = col <= row
    tril = causal.astype(f32)
    gn = gn_ref[...]

    for c in range(GLA_STEP // C):
        rows = slice(c * C, (c + 1) * C)
        q = x_ref[rows, 0:GLA_QK].astype(f32)
        k = x_ref[rows, GLA_QK:2 * GLA_QK].astype(f32)
        g = glog_ref[rows, :]
        b = jnp.dot(tril, g, preferred_element_type=f32, precision=lax.Precision.HIGHEST)
        b_last = b[C - 1:C, :]
        anchor = 0.5 * b_last
        q_inter = (q * jnp.exp(b)).astype(bf16)
        q_s = (q * jnp.exp(b - anchor)).astype(bf16)
        k_s = (k * jnp.exp(anchor - b)).astype(bf16)
        k_d = (k * jnp.exp(b_last - b)).astype(bf16)
        decay = jnp.exp(b_last)
        state = state_ref[...]
        state_b = state.astype(bf16)
        new_cols = []
        for h in range(GLA_HEADS):
            hs = slice(h * GLA_DK, (h + 1) * GLA_DK)
            vs = slice(2 * GLA_QK + h * GLA_DV, 2 * GLA_QK + (h + 1) * GLA_DV)
            gs = slice(2 * GLA_QK + GLA_V + h * GLA_DV, 2 * GLA_QK + GLA_V + (h + 1) * GLA_DV)
            v = x_ref[rows, vs]
            attn = jnp.where(causal, _dot_nt(q_s[:, hs], k_s[:, hs]), 0.0).astype(bf16)
            o = _dot(attn, v) + _dot_nt(q_inter[:, hs], state_b[:, hs])
            new_cols.append(_dot_tn(v, k_d[:, hs]))
            gg = x_ref[rows, gs].astype(f32)
            o_ref[rows, h * GLA_DV:(h + 1) * GLA_DV] = (
                _rms(o, gn) * (gg * jax.nn.sigmoid(gg))).astype(o_ref.dtype)
        state_ref[...] = state * decay + jnp.concatenate(new_cols, axis=1)


def _gla(gla_in, glog, gn, B, S):
    T = gla_in.shape[0]
    step = GLA_STEP
    nS = S // step
    return pl.pallas_call(
        _gla_kernel,
        grid=(B, nS),
        in_specs=[pl.BlockSpec((step, 1536), lambda b, s: (b * nS + s, 0)),
                  pl.BlockSpec((step, GLA_QK), lambda b, s: (b * nS + s, 0)),
                  _const_spec((1, GLA_DV))],
        out_specs=pl.BlockSpec((step, GLA_V), lambda b, s: (b * nS + s, 0)),
        out_shape=jax.ShapeDtypeStruct((T, GLA_V), bf16),
        scratch_shapes=[pltpu.VMEM((GLA_DV, GLA_QK), f32)],
        compiler_params=pltpu.CompilerParams(dimension_semantics=("arbitrary", "arbitrary"),
                                             vmem_limit_bytes=VMEM_LIMIT_BYTES),
        name="gla",
    )(gla_in, glog, gn)


def _sortable(x):
    bits = lax.bitcast_convert_type(x + 0.0, jnp.int32)
    return bits ^ ((bits >> 31) & jnp.int32(0x7FFFFFFF))


def _dsa_kernel(topk, pos_bits, q_ref, k_ref, vt_ref, iq_ref, ikk_ref, iw_ref, bias_ref,
                o_ref, keys_ref, hi_ref, lo_ref, madd_ref, acc_ref, wiq_ref, wq_ref, p_ref, mt_ref,
                st_ref):
    QB = DSA_QB
    G = QB // SUBLANES
    qi = pl.program_id(1)
    n_tiles = qi + 1
    idx_scale = (IDX_HEADS ** -0.5) * (IDX_DIM ** -0.5)

    row_half = lax.broadcasted_iota(jnp.int32, (LANES, QB), 0) // DSA_DH
    key_j = lax.broadcasted_iota(jnp.int32, (QB, QB), 0)
    qry_i = lax.broadcasted_iota(jnp.int32, (QB, QB), 1)
    tri = key_j <= qry_i

    def stage_head_operands(src_ref, dst_ref):
        t = src_ref[...].astype(f32).T
        for h in range(DSA_HEADS):
            pair = t[(h // 2) * LANES:(h // 2 + 1) * LANES, :]
            dst_ref[h] = jnp.where(row_half == (h % 2), pair, 0.0).astype(bf16)

    def group_sum(x):
        return jnp.sum(x.reshape(G, SUBLANES, QB), axis=0)

    def group_max(x):
        return jnp.max(x.reshape(G, SUBLANES, QB), axis=0)

    def tile_rows(kt):
        return pl.ds(pl.multiple_of(kt * QB, QB), QB)

    n_pairs = (n_tiles + 1) // 2

    def pad_odd_tile(ref):
        @pl.when(n_tiles % 2 == 1)
        def _():
            ref[tile_rows(n_tiles), :] = jnp.full((QB, QB), -2 ** 15, jnp.int16)

    w_t = iw_ref[...].T
    stage_head_operands(iq_ref, wiq_ref)

    def score_tile(kt, carry):
        kk = ikk_ref[tile_rows(kt), :]
        acc = jnp.zeros((QB, QB), f32)
        for h in range(IDX_HEADS):
            acc = acc + jnp.maximum(_dot(kk, wiq_ref[h]), 0.0) * w_t[h:h + 1, :]
        key = _sortable(acc * idx_scale)
        visible = jnp.logical_or(kt < qi, tri)
        key = jnp.where(visible, key, INT_MIN)
        keys_ref[tile_rows(kt), :] = key
        hi_ref[tile_rows(kt), :] = (key >> 16).astype(jnp.int16)
        return carry

    lax.fori_loop(0, n_tiles, score_tile, 0)
    pad_odd_tile(hi_ref)

    def count(pred):
        def body(kt, cnt):
            return cnt + group_sum(jnp.where(pred(keys_ref[tile_rows(kt), :], kt), 1, 0))
        cnt = lax.fori_loop(0, n_tiles, body, jnp.zeros((SUBLANES, QB), jnp.int32))
        return jnp.sum(cnt, axis=0, keepdims=True)

    PACK = 2 * SUBLANES
    one16 = jnp.ones((), jnp.int16)
    zero16 = jnp.zeros((), jnp.int16)

    def count16(ref, pred):
        def body(kp, cnt):
            blk = ref[pl.ds(pl.multiple_of(kp * 2 * QB, 2 * QB), 2 * QB), :]
            c3 = jnp.where(pred(blk), one16, zero16).reshape(2 * QB // PACK, PACK, QB)
            parts = [c3[g] for g in range(4)]
            for g in range(4, 2 * QB // PACK):
                parts[g % 4] = parts[g % 4] + c3[g]
            return cnt + ((parts[0] + parts[1]) + (parts[2] + parts[3]))
        cnt = lax.fori_loop(0, n_pairs, body, jnp.zeros((PACK, QB), jnp.int16))
        return jnp.sum(cnt.astype(jnp.int32), axis=0, keepdims=True)

    zero_row = jnp.zeros((1, QB), jnp.int32)

    def count_hi_ge(trial):
        t16 = trial.astype(jnp.int16)
        return count16(hi_ref, lambda blk: blk >= t16)

    a1 = jnp.where(count_hi_ge(zero_row) >= topk, zero_row, zero_row - 2 ** 15)

    def hi_step(i, a1):
        trial = a1 | (jnp.int32(1) << (14 - i))
        return jnp.where(count_hi_ge(trial) >= topk, trial, a1)

    a1 = lax.fori_loop(0, 15, hi_step, a1)
    a1_16 = a1.astype(jnp.int16)
    k2 = topk - count16(hi_ref, lambda blk: blk > a1_16)

    def low_tile(kt, carry):
        key = keys_ref[tile_rows(kt), :]
        low = jnp.where((key >> 16) == a1, (key & 0xFFFF) - 2 ** 15, -2 ** 15)
        lo_ref[tile_rows(kt), :] = low.astype(jnp.int16)
        return carry

    lax.fori_loop(0, n_tiles, low_tile, 0)
    pad_odd_tile(lo_ref)

    def lo_step(i, a2):
        trial = a2 | (jnp.int32(1) << (15 - i))
        t16 = (trial - 2 ** 15).astype(jnp.int16)
        return jnp.where(count16(lo_ref, lambda blk: blk >= t16) >= k2, trial, a2)

    a2 = lax.fori_loop(0, 16, lo_step, zero_row)
    ans = (a1 << 16) | a2
    thr = jnp.maximum(ans, INT_MIN + 1)

    a2_16 = (a2 - 2 ** 15).astype(jnp.int16)
    c_ge = (topk - k2) + count16(lo_ref, lambda blk: blk >= a2_16)
    tied = jnp.logical_and(c_ge > topk, ans > INT_MIN)
    any_tied = jnp.max(jnp.where(tied, 1, 0)) > 0

    def pos_of(kt):
        return key_j + kt * QB

    def tie_cutoff():
        need = topk - count(lambda blk, kt: blk > thr)

        def pbit(i, cut):
            trial = cut | (jnp.int32(1) << (pos_bits - 1 - i))
            c = count(lambda blk, kt: jnp.logical_and(blk == thr, pos_of(kt) < trial))
            return jnp.where(c < need, trial, cut)

        cut = lax.fori_loop(0, pos_bits, pbit, zero_row)
        return jnp.where(tied, cut, jnp.int32(2 ** 30))

    cut = lax.cond(any_tied, tie_cutoff, lambda: zero_row + jnp.int32(2 ** 30))

    def mask_tile(kt, carry):
        blk = keys_ref[tile_rows(kt), :]
        sel = jnp.logical_or(blk > thr, jnp.logical_and(blk == thr, pos_of(kt) <= cut))
        madd_ref[tile_rows(kt), :] = jnp.where(sel, 0.0, NEG_BIG).astype(f32)
        return carry

    lax.fori_loop(0, n_tiles, mask_tile, 0)

    stage_head_operands(q_ref, wq_ref)

    def prob_tile(near, kt, m_run):
        out = []
        for h in range(DSA_HEADS):
            kp = k_ref[tile_rows(kt), (h // 2) * LANES:(h // 2 + 1) * LANES]
            l = _dot(kp, wq_ref[h]) + madd_ref[tile_rows(kt), :]
            if near:
                l = l + bias_ref[(qi - kt) * DSA_HEADS + h]
            m_t = jnp.max(group_max(l), axis=0, keepdims=True)
            p = jnp.exp2(l - m_t)
            p_ref[h, tile_rows(kt), :] = p.astype(bf16)
            mt_ref[kt * DSA_HEADS + h] = m_t
            st_ref[kt * DSA_HEADS + h] = jnp.sum(group_sum(p), axis=0, keepdims=True)
            out.append(jnp.maximum(m_run[h], m_t))
        return tuple(out)

    n_far = jnp.maximum(qi - 1, 0)
    m_all = tuple(jnp.full((1, QB), NEG_BIG, f32) for _ in range(DSA_HEADS))
    m_all = lax.fori_loop(0, n_far, functools.partial(prob_tile, False), m_all)
    m_all = lax.fori_loop(n_far, n_tiles, functools.partial(prob_tile, True), m_all)

    acc_ref[...] = jnp.zeros_like(acc_ref)

    def pv_tile(kt, ss):
        out = []
        for h in range(DSA_HEADS):
            w = jnp.exp2(mt_ref[kt * DSA_HEADS + h] - m_all[h])
            out.append(ss[h] + w * st_ref[kt * DSA_HEADS + h])
            vt = vt_ref[h * DSA_DH:(h + 1) * DSA_DH, tile_rows(kt)]
            acc_ref[h * DSA_DH:(h + 1) * DSA_DH, :] += _dot(vt, p_ref[h, tile_rows(kt), :]) * w
        return tuple(out)

    ss = lax.fori_loop(0, n_tiles, pv_tile,
                       tuple(jnp.zeros((1, QB), f32) for _ in range(DSA_HEADS)))
    for h in range(DSA_HEADS):
        acc_ref[h * DSA_DH:(h + 1) * DSA_DH, :] = acc_ref[h * DSA_DH:(h + 1) * DSA_DH, :] * (1.0 / ss[h])
    o_ref[...] = acc_ref[...].T.astype(o_ref.dtype)


def _dsa(dqk, vt, iq, ikk, iw, bias_tiles, B, S):
    T = dqk.shape[0]
    QB = DSA_QB
    nq = S // QB
    assert nq % 2 == 0
    topk = min(TOPK_MAX, S // 4)
    pos_bits = max(1, int(math.ceil(math.log2(S))))
    kern = functools.partial(_dsa_kernel, topk, pos_bits)
    return pl.pallas_call(
        kern,
        grid=(B, nq),
        in_specs=[
            pl.BlockSpec((QB, DSA_W), lambda b, i: (b * nq + i, 0)),
            pl.BlockSpec((S, DSA_W), lambda b, i: (b, 1), pipeline_mode=pl.Buffered(1)),
            pl.BlockSpec((None, DSA_W, S), lambda b, i: (b, 0, 0), pipeline_mode=pl.Buffered(1)),
            pl.BlockSpec((QB, IDX_Q), lambda b, i: (b * nq + i, 0)),
            pl.BlockSpec((S, LANES), lambda b, i: (b, 0), pipeline_mode=pl.Buffered(1)),
            pl.BlockSpec((QB, LANES), lambda b, i: (b * nq + i, 0)),
            _const_spec(bias_tiles.shape),
        ],
        out_specs=pl.BlockSpec((QB, DSA_W), lambda b, i: (b * nq + i, 0)),
        out_shape=jax.ShapeDtypeStruct((T, DSA_W), bf16),
        scratch_shapes=[pltpu.VMEM((S, QB), jnp.int32),
                        pltpu.VMEM((S, QB), jnp.int16),
                        pltpu.VMEM((S, QB), jnp.int16),
                        pltpu.VMEM((S, QB), f32),
                        pltpu.VMEM((DSA_W, QB), f32),
                        pltpu.VMEM((IDX_HEADS, LANES, QB), bf16),
                        pltpu.VMEM((DSA_HEADS, LANES, QB), bf16),
                        pltpu.VMEM((DSA_HEADS, S, QB), bf16),
                        pltpu.VMEM((nq * DSA_HEADS, 1, QB), f32),
                        pltpu.VMEM((nq * DSA_HEADS, 1, QB), f32)],
        compiler_params=pltpu.CompilerParams(dimension_semantics=("arbitrary", "arbitrary"),
                                             vmem_limit_bytes=VMEM_LIMIT_BYTES),
        name="dsa",
    )(dqk, dqk, vt, iq, ikk, iw, bias_tiles)


def _rel_bucket(dist):
    max_exact = REL_BUCKETS // 2
    d = jnp.maximum(dist, 1).astype(f32)
    large = max_exact + (jnp.log(d / max_exact) / math.log(REL_MAX_DIST / max_exact)
                         * (REL_BUCKETS - max_exact)).astype(jnp.int32)
    large = jnp.minimum(large, REL_BUCKETS - 1)
    return jnp.where(dist < max_exact, dist, large)


def _bias_tiles(rel_bias):
    QB = DSA_QB
    j = jnp.arange(QB, dtype=jnp.int32)[:, None]
    i = jnp.arange(QB, dtype=jnp.int32)[None, :]
    far = rel_bias[REL_BUCKETS - 1].astype(f32)
    tiles = []
    assert 2 * QB - (QB - 1) >= REL_MAX_DIST
    for rel in range(2):
        dist = jnp.maximum(rel * QB + i - j, 0)
        onehot = (_rel_bucket(dist)[None] == jnp.arange(REL_BUCKETS, dtype=jnp.int32)[:, None, None])
        tiles.append(jnp.einsum('bh,bji->hji', (rel_bias.astype(f32) - far[None, :]) * LOG2E,
                                onehot.astype(f32), precision=lax.Precision.HIGHEST))
    return jnp.concatenate(tiles, axis=0)


def _mix_kernel(h_ref, oa_ref, ob_ref, gate_ref, wa_ref, wb_ref, wo_ref, gain_ref, out_ref):
    D = h_ref.shape[1]
    ya = _dot(oa_ref[...], wa_ref[...])
    yb = _dot(ob_ref[...], wb_ref[...])
    ga = jax.nn.sigmoid(gate_ref[:, 0:D].astype(f32))
    gb = jax.nn.sigmoid(gate_ref[:, D:2 * D].astype(f32))
    mixed = (ga * ya + gb * yb).astype(bf16)
    m = _dot(mixed, wo_ref[...])
    out_ref[...] = h_ref[...] + _rms(m, gain_ref[...])


def _mix(h2d, oa, ob, gates, wa, wb, wo, gain):
    T, D = h2d.shape
    tm = min(TOKEN_TILE, T)
    tok = lambda n: pl.BlockSpec((tm, n), lambda t: (t, 0))
    return pl.pallas_call(
        _mix_kernel,
        grid=(T // tm,),
        in_specs=[tok(D), tok(GLA_V), tok(DSA_W), tok(2 * D),
                  _const_spec(wa.shape), _const_spec(wb.shape), _const_spec(wo.shape),
                  _const_spec((1, D))],
        out_specs=tok(D),
        out_shape=jax.ShapeDtypeStruct((T, D), f32),
        compiler_params=pltpu.CompilerParams(dimension_semantics=("arbitrary",),
                                             vmem_limit_bytes=VMEM_LIMIT_BYTES),
        name="mix",
    )(h2d, oa, ob, gates, wa, wb, wo, gain)


def _mlp_kernel(h_ref, p_ref, w1_ref, w2_ref, wple_ref, wg_ref, g_pre_ref, g_post_ref, g_ple_ref,
                out_ref, acc_ref):
    h1 = h_ref[...]
    um = _rms(h1, g_pre_ref[...]).astype(bf16)
    d_ff = w1_ref.shape[1]
    for c in range(d_ff // FF_CHUNK):
        cs = slice(c * FF_CHUNK, (c + 1) * FF_CHUNK)
        hid = jnp.maximum(_dot(um, w1_ref[:, cs]), 0.0)
        part = _dot((hid * hid).astype(bf16), w2_ref[cs, :])
        if c == 0:
            acc_ref[...] = part
        else:
            acc_ref[...] += part
    h2 = h1 + _rms(acc_ref[...], g_post_ref[...])
    e = _dot(p_ref[...].astype(bf16), wple_ref[...]) * jax.nn.sigmoid(_dot(h2.astype(bf16), wg_ref[...]))
    out_ref[...] = h2 + _rms(e, g_ple_ref[...])


def _mlp(h2d, p2d, w1, w2, wple, wg, g_pre, g_post, g_ple):
    T, D = h2d.shape
    tm = min(TOKEN_TILE, T)
    tok = lambda n: pl.BlockSpec((tm, n), lambda t: (t, 0))
    return pl.pallas_call(
        _mlp_kernel,
        grid=(T // tm,),
        in_specs=[tok(D), tok(p2d.shape[1]),
                  _const_spec(w1.shape), _const_spec(w2.shape), _const_spec(wple.shape),
                  _const_spec(wg.shape), _const_spec((1, D)), _const_spec((1, D)), _const_spec((1, D))],
        out_specs=tok(D),
        out_shape=jax.ShapeDtypeStruct((T, D), f32),
        scratch_shapes=[pltpu.VMEM((tm, D), f32)],
        compiler_params=pltpu.CompilerParams(dimension_semantics=("arbitrary",),
                                             vmem_limit_bytes=VMEM_LIMIT_BYTES),
        name="mlp",
    )(h2d, p2d, w1, w2, wple, wg, g_pre, g_post, g_ple)


def _regroup_w_in(w, d_model):
    sizes = (GLA_QK, GLA_QK, GLA_V, GLA_V, GLA_GATE_RANK, DSA_W, DSA_W, DSA_W,
             IDX_Q, IDX_DIM, IDX_HEADS, d_model, d_model)
    pts = np.cumsum(sizes)[:-1].tolist()
    gq, gk, gv, gg, glr, dq, dk, dv, iq, ik, iw, ga, gb = jnp.split(w, pts, axis=1)
    pad = lambda a: jnp.pad(a, ((0, 0), (0, LANES - a.shape[1])))
    w_all = jnp.concatenate(
        [gq * (GLA_DK ** -0.5), gk, gv, gg, dq * (DSA_DH ** -0.5 * LOG2E), dk, iq, ik, ik, ga, gb,
         pad(glr), pad(iw)], axis=1).astype(bf16)
    assert w_all.shape[1] == _P_TOTAL
    return w_all, dv.T.astype(bf16)


def kernel(x, p, rel_bias, ln_mix_pre, ln_mix_post, ln_mlp_pre, ln_mlp_post, ln_ple_post, w_in,
           gla_gate_w2, gla_gate_b, gla_norm, w_branch_a, w_branch_b, w_out, w_mlp_in, w_mlp_out,
           w_ple, w_ple_gate):
    B, S, D = x.shape
    depth = w_in.shape[0]
    T = B * S
    assert S % DSA_QB == 0 and S % GLA_STEP == 0 and T % TOKEN_TILE == 0
    bias_tiles = _bias_tiles(rel_bias)
    h = x.reshape(T, D)
    row = lambda a: a.reshape(1, -1).astype(f32)
    for i in range(depth):
        w_all, w_vt = _regroup_w_in(w_in[i], D)
        w2p = jnp.pad(gla_gate_w2[i], ((0, LANES - GLA_GATE_RANK), (0, 0))).astype(bf16)
        gla_in, glog, dqk, vt, iq, ikk, iw, gates = _proj(
            h, row(ln_mix_pre[i]), w_all, w_vt, w2p, row(gla_gate_b[i]), B, S)
        o_a = _gla(gla_in, glog, row(gla_norm[i]), B, S)
        o_b = _dsa(dqk, vt, iq, ikk, iw, bias_tiles, B, S)
        h = _mix(h, o_a, o_b, gates, w_branch_a[i].astype(bf16), w_branch_b[i].astype(bf16),
                 w_out[i].astype(bf16), row(ln_mix_post[i]))
        h = _mlp(h, p[i].reshape(T, -1), w_mlp_in[i].astype(bf16), w_mlp_out[i].astype(bf16),
                 w_ple[i].astype(bf16), w_ple_gate[i].astype(bf16),
                 row(ln_mlp_pre[i]), row(ln_mlp_post[i]), row(ln_ple_post[i]))
    return h.reshape(B, S, D)
```

```python
import functools
import math

import numpy as np
import jax
import jax.numpy as jnp
from jax import lax
from jax.experimental import pallas as pl
from jax.experimental.pallas import tpu as pltpu

GLA_HEADS = 4
GLA_DK = 64
GLA_DV = 128
GLA_GATE_RANK = 16
GLA_GATE_NORM = 16.0
GLA_CHUNK = 64
DSA_HEADS = 8
DSA_DH = 64
IDX_HEADS = 8
IDX_DIM = 64
TOPK_MAX = 256
REL_BUCKETS = 32
REL_MAX_DIST = 128
EPS = 1e-6

GLA_QK = GLA_HEADS * GLA_DK
GLA_V = GLA_HEADS * GLA_DV
DSA_W = DSA_HEADS * DSA_DH
IDX_Q = IDX_HEADS * IDX_DIM

LANES = 128
SUBLANES = 8
VMEM_LIMIT_BYTES = 56 * 1024 * 1024

TOKEN_TILE = 512
GLA_STEP = 256
DSA_QB = 256
DSA_ACC_ROWS = DSA_DH + 2 * SUBLANES
FF_CHUNK = 512

LOG2E = math.log2(math.e)
INT_MIN = np.int32(-2 ** 31)
NEG_BIG = -1e30

bf16 = jnp.bfloat16
f32 = jnp.float32


def _const_spec(shape):
    nd = len(shape)
    return pl.BlockSpec(shape, lambda *_: (0,) * nd, pipeline_mode=pl.Buffered(1))


def _rms(xf, gain):
    return xf * lax.rsqrt(jnp.mean(xf * xf, axis=-1, keepdims=True) + EPS) * gain


def _dot(a, b):
    return jnp.dot(a, b, preferred_element_type=f32)


def _dot_nt(a, b):
    return lax.dot_general(a, b, (((1,), (1,)), ((), ())), preferred_element_type=f32)


def _dot_tn(a, b):
    return lax.dot_general(a, b, (((0,), (0,)), ((), ())), preferred_element_type=f32)


_P_GLA = (0, 1536)
_P_DQK = (1536, 2560)
_P_IQ = (2560, 3072)
_P_IKK = (3072, 3200)
_P_GATE = (3200, 5248)
_P_GLR = (5248, 5376)
_P_IW = (5376, 5504)
_P_TOTAL = 5504


def _proj_kernel(h_ref, gain_ref, w_ref, wvt_ref, w2_ref, gb_ref,
                 gla_ref, glog_ref, dqk_ref, vt_ref, iq_ref, ikk_ref, iw_ref, gate_ref):
    u = _rms(h_ref[...], gain_ref[...]).astype(bf16)

    def proj(span):
        return _dot(u, w_ref[:, span[0]:span[1]])

    gla_ref[...] = proj(_P_GLA).astype(bf16)
    dqk_ref[...] = proj(_P_DQK).astype(bf16)
    iq_ref[...] = proj(_P_IQ).astype(bf16)
    ikk_ref[...] = proj(_P_IKK).astype(bf16)
    gate_ref[...] = proj(_P_GATE).astype(bf16)
    iw_ref[...] = proj(_P_IW)
    vt_ref[...] = _dot_nt(wvt_ref[...], u).astype(bf16)
    glr = proj(_P_GLR).astype(bf16)
    x = _dot(glr, w2_ref[...]) + gb_ref[...]
    ls = jnp.minimum(x, 0.0) - jnp.log(1.0 + jnp.exp(-jnp.abs(x)))
    glog_ref[...] = ls * (1.0 / GLA_GATE_NORM)


def _proj(h2d, gain, w_all, w_vt, w2p, gb, B, S):
    T, D = h2d.shape
    tm = min(TOKEN_TILE, S)
    nS = S // tm
    tok = lambda n: pl.BlockSpec((tm, n), lambda t: (t, 0))
    out_shape = (
        jax.ShapeDtypeStruct((T, 1536), bf16),
        jax.ShapeDtypeStruct((T, GLA_QK), f32),
        jax.ShapeDtypeStruct((T, 2 * DSA_W), bf16),
        jax.ShapeDtypeStruct((B, DSA_W, S), bf16),
        jax.ShapeDtypeStruct((T, IDX_Q), bf16),
        jax.ShapeDtypeStruct((T, LANES), bf16),
        jax.ShapeDtypeStruct((T, LANES), f32),
        jax.ShapeDtypeStruct((T, 2 * D), bf16),
    )
    out_specs = (
        tok(1536), tok(GLA_QK), tok(2 * DSA_W),
        pl.BlockSpec((None, DSA_W, tm), lambda t: (t // nS, 0, t % nS)),
        tok(IDX_Q), tok(LANES), tok(LANES), tok(2 * D),
    )
    return pl.pallas_call(
        _proj_kernel,
        grid=(T // tm,),
        in_specs=[tok(D), _const_spec((1, D)), _const_spec(w_all.shape), _const_spec(w_vt.shape),
                  _const_spec(w2p.shape), _const_spec((1, GLA_QK))],
        out_specs=out_specs,
        out_shape=out_shape,
        compiler_params=pltpu.CompilerParams(dimension_semantics=("arbitrary",),
                                             vmem_limit_bytes=VMEM_LIMIT_BYTES),
        name="proj",
    )(h2d, gain, w_all, w_vt, w2p, gb)


def _gla_kernel(x_ref, glog_ref, gn_ref, o_ref, state_ref):
    @pl.when(pl.program_id(1) == 0)
    def _():
        state_ref[...] = jnp.zeros_like(state_ref)

    C = GLA_CHUNK
    row = lax.broadcasted_iota(jnp.int32, (C, C), 0)
    col = lax.broadcasted_iota(jnp.int32, (C, C), 1)
    causal = col <= row
    tril = causal.astype(f32)
    gn = gn_ref[...]

    for c in range(GLA_STEP // C):
        rows = slice(c * C, (c + 1) * C)
        q = x_ref[rows, 0:GLA_QK].astype(f32)
        k = x_ref[rows, GLA_QK:2 * GLA_QK].astype(f32)
        g = glog_ref[rows, :]
        b = jnp.dot(tril, g, preferred_element_type=f32, precision=lax.Precision.HIGHEST)
        b_last = b[C - 1:C, :]
        anchor = 0.5 * b_last
        q_inter = (q * jnp.exp(b)).astype(bf16)
        q_s = (q * jnp.exp(b - anchor)).astype(bf16)
        k_s = (k * jnp.exp(anchor - b)).astype(bf16)
        k_d = (k * jnp.exp(b_last - b)).astype(bf16)
        decay = jnp.exp(b_last)
        state = state_ref[...]
        state_b = state.astype(bf16)
        new_cols = []
        for h in range(GLA_HEADS):
            hs = slice(h * GLA_DK, (h + 1) * GLA_DK)
            vs = slice(2 * GLA_QK + h * GLA_DV, 2 * GLA_QK + (h + 1) * GLA_DV)
            gs = slice(2 * GLA_QK + GLA_V + h * GLA_DV, 2 * GLA_QK + GLA_V + (h + 1) * GLA_DV)
            v = x_ref[rows, vs]
            attn = jnp.where(causal, _dot_nt(q_s[:, hs], k_s[:, hs]), 0.0).astype(bf16)
            o = _dot(attn, v) + _dot_nt(q_inter[:, hs], state_b[:, hs])
            new_cols.append(_dot_tn(v, k_d[:, hs]))
            gg = x_ref[rows, gs].astype(f32)
            o_ref[rows, h * GLA_DV:(h + 1) * GLA_DV] = (
                _rms(o, gn) * (gg * jax.nn.sigmoid(gg))).astype(o_ref.dtype)
        state_ref[...] = state * decay + jnp.concatenate(new_cols, axis=1)


def _gla(gla_in, glog, gn, B, S):
    T = gla_in.shape[0]
    step = GLA_STEP
    nS = S // step
    return pl.pallas_call(
        _gla_kernel,
        grid=(B, nS),
        in_specs=[pl.BlockSpec((step, 1536), lambda b, s: (b * nS + s, 0)),
                  pl.BlockSpec((step, GLA_QK), lambda b, s: (b * nS + s, 0)),
                  _const_spec((1, GLA_DV))],
        out_specs=pl.BlockSpec((step, GLA_V), lambda b, s: (b * nS + s, 0)),
        out_shape=jax.ShapeDtypeStruct((T, GLA_V), bf16),
        scratch_shapes=[pltpu.VMEM((GLA_DV, GLA_QK), f32)],
        compiler_params=pltpu.CompilerParams(dimension_semantics=("arbitrary", "arbitrary"),
                                             vmem_limit_bytes=VMEM_LIMIT_BYTES),
        name="gla",
    )(gla_in, glog, gn)


def _sortable(x):
    bits = lax.bitcast_convert_type(x + 0.0, jnp.int32)
    return bits ^ ((bits >> 31) & jnp.int32(0x7FFFFFFF))


def _dsa_kernel(topk, pos_bits, q_ref, k_ref, vt_ref, iq_ref, ikk_ref, iw_ref, bias_ref,
                o_ref, keys_ref, hi_ref, lo_ref, madd_ref, acc_ref, wiq_ref, wq_ref,
                l0_ref, l1_ref, p0_ref, p1_ref, ml_ref, rt_ref):
    QB = DSA_QB
    G = QB // SUBLANES
    qi = pl.program_id(1)
    n_tiles = qi + 1
    idx_scale = (IDX_HEADS ** -0.5) * (IDX_DIM ** -0.5)

    row_half = lax.broadcasted_iota(jnp.int32, (LANES, QB), 0) // DSA_DH
    key_j = lax.broadcasted_iota(jnp.int32, (QB, QB), 0)
    qry_i = lax.broadcasted_iota(jnp.int32, (QB, QB), 1)
    tri = key_j <= qry_i

    def stage_head_operands(src_ref, dst_ref):
        t = src_ref[...].astype(f32).T
        for h in range(DSA_HEADS):
            pair = t[(h // 2) * LANES:(h // 2 + 1) * LANES, :]
            dst_ref[h] = jnp.where(row_half == (h % 2), pair, 0.0).astype(bf16)

    def group_sum(x):
        return jnp.sum(x.reshape(G, SUBLANES, QB), axis=0)

    def group_max(x):
        return jnp.max(x.reshape(G, SUBLANES, QB), axis=0)

    def tile_rows(kt):
        return pl.ds(pl.multiple_of(kt * QB, QB), QB)

    n_pairs = (n_tiles + 1) // 2

    def pad_odd_tile(ref):
        @pl.when(n_tiles % 2 == 1)
        def _():
            ref[tile_rows(n_tiles), :] = jnp.full((QB, QB), -2 ** 15, jnp.int16)

    w_t = iw_ref[...].T
    stage_head_operands(iq_ref, wiq_ref)

    def score_tile(kt, carry):
        kk = ikk_ref[tile_rows(kt), :]
        acc = jnp.zeros((QB, QB), f32)
        for h in range(IDX_HEADS):
            acc = acc + jnp.maximum(_dot(kk, wiq_ref[h]), 0.0) * w_t[h:h + 1, :]
        key = _sortable(acc * idx_scale)
        visible = jnp.logical_or(kt < qi, tri)
        key = jnp.where(visible, key, INT_MIN)
        keys_ref[tile_rows(kt), :] = key
        hi_ref[tile_rows(kt), :] = (key >> 16).astype(jnp.int16)
        return carry

    lax.fori_loop(0, n_tiles, score_tile, 0)
    pad_odd_tile(hi_ref)

    def count(pred):
        def body(kt, cnt):
            return cnt + group_sum(jnp.where(pred(keys_ref[tile_rows(kt), :], kt), 1, 0))
        cnt = lax.fori_loop(0, n_tiles, body, jnp.zeros((SUBLANES, QB), jnp.int32))
        return jnp.sum(cnt, axis=0, keepdims=True)

    PACK = 2 * SUBLANES
    one16 = jnp.ones((), jnp.int16)
    zero16 = jnp.zeros((), jnp.int16)

    def count16(ref, pred):
        def body(kp, cnt):
            blk = ref[pl.ds(pl.multiple_of(kp * 2 * QB, 2 * QB), 2 * QB), :]
            c3 = jnp.where(pred(blk), one16, zero16).reshape(2 * QB // PACK, PACK, QB)
            parts = [c3[g] for g in range(4)]
            for g in range(4, 2 * QB // PACK):
                parts[g % 4] = parts[g % 4] + c3[g]
            return cnt + ((parts[0] + parts[1]) + (parts[2] + parts[3]))
        cnt = lax.fori_loop(0, n_pairs, body, jnp.zeros((PACK, QB), jnp.int16))
        return jnp.sum(cnt.astype(jnp.int32), axis=0, keepdims=True)

    zero_row = jnp.zeros((1, QB), jnp.int32)

    def count_hi_ge(trial):
        t16 = trial.astype(jnp.int16)
        return count16(hi_ref, lambda blk: blk >= t16)

    a1 = jnp.where(count_hi_ge(zero_row) >= topk, zero_row, zero_row - 2 ** 15)

    def hi_step(i, a1):
        trial = a1 | (jnp.int32(1) << (14 - i))
        return jnp.where(count_hi_ge(trial) >= topk, trial, a1)

    a1 = lax.fori_loop(0, 15, hi_step, a1)
    a1_16 = a1.astype(jnp.int16)
    k2 = topk - count16(hi_ref, lambda blk: blk > a1_16)

    def low_tile(kt, carry):
        key = keys_ref[tile_rows(kt), :]
        low = jnp.where((key >> 16) == a1, (key & 0xFFFF) - 2 ** 15, -2 ** 15)
        lo_ref[tile_rows(kt), :] = low.astype(jnp.int16)
        return carry

    lax.fori_loop(0, n_tiles, low_tile, 0)
    pad_odd_tile(lo_ref)

    def lo_step(i, a2):
        trial = a2 | (jnp.int32(1) << (15 - i))
        t16 = (trial - 2 ** 15).astype(jnp.int16)
        return jnp.where(count16(lo_ref, lambda blk: blk >= t16) >= k2, trial, a2)

    a2 = lax.fori_loop(0, 16, lo_step, zero_row)
    ans = (a1 << 16) | a2
    thr = jnp.maximum(ans, INT_MIN + 1)

    a2_16 = (a2 - 2 ** 15).astype(jnp.int16)
    c_ge = (topk - k2) + count16(lo_ref, lambda blk: blk >= a2_16)
    tied = jnp.logical_and(c_ge > topk, ans > INT_MIN)
    any_tied = jnp.max(jnp.where(tied, 1, 0)) > 0

    def pos_of(kt):
        return key_j + kt * QB

    def tie_cutoff():
        need = topk - count(lambda blk, kt: blk > thr)

        def pbit(i, cut):
            trial = cut | (jnp.int32(1) << (pos_bits - 1 - i))
            c = count(lambda blk, kt: jnp.logical_and(blk == thr, pos_of(kt) < trial))
            return jnp.where(c < need, trial, cut)

        cut = lax.fori_loop(0, pos_bits, pbit, zero_row)
        return jnp.where(tied, cut, jnp.int32(2 ** 30))

    cut = lax.cond(any_tied, tie_cutoff, lambda: zero_row + jnp.int32(2 ** 30))

    def mask_tile(kt, carry):
        blk = keys_ref[tile_rows(kt), :]
        sel = jnp.logical_or(blk > thr, jnp.logical_and(blk == thr, pos_of(kt) <= cut))
        madd_ref[tile_rows(kt), :] = jnp.where(sel, 0.0, NEG_BIG).astype(f32)
        return carry

    lax.fori_loop(0, n_tiles, mask_tile, 0)

    stage_head_operands(q_ref, wq_ref)

    @pl.when(n_tiles % 2 == 1)
    def _():
        madd_ref[tile_rows(n_tiles), :] = jnp.full((QB, QB), NEG_BIG, f32)

    @pl.when(jnp.logical_and(pl.program_id(0) == 0, qi == 0))
    def _():
        p0_ref[...] = jnp.zeros_like(p0_ref)
        p1_ref[...] = jnp.zeros_like(p1_ref)

    l_refs = (l0_ref, l1_ref)
    p_refs = (p0_ref, p1_ref)
    last_tile = 2 * n_pairs - 1
    acc_rows = lambda h: slice(h * DSA_ACC_ROWS, (h + 1) * DSA_ACC_ROWS)
    ones_rows = jnp.ones((DSA_ACC_ROWS - DSA_DH, QB), bf16)

    def stage_logits(near, t, slot):
        if near:
            t = jnp.minimum(t, last_tile)
            rel = qi - t
            bias_idx = jnp.where(rel < 0, 2, jnp.minimum(rel, 2)) * DSA_HEADS
        for h in range(DSA_HEADS):
            kp = k_ref[tile_rows(t), (h // 2) * LANES:(h // 2 + 1) * LANES]
            l = _dot(kp, wq_ref[h]) + madd_ref[tile_rows(t), :]
            if near:
                l = l + bias_ref[bias_idx + h]
            l_refs[slot][h] = l
            ml_ref[slot * DSA_HEADS + h] = jnp.max(group_max(l), axis=0, keepdims=True)

    def stage_probs(slot, run_max):
        new_max = []
        for h in range(DSA_HEADS):
            r = jnp.maximum(run_max[h], ml_ref[slot * DSA_HEADS + h])
            p_refs[slot][h] = jnp.exp2(l_refs[slot][h] - r).astype(bf16)
            rt_ref[slot * DSA_HEADS + h] = r
            new_max.append(r)
        return tuple(new_max)

    def stage_pv(t, slot, prev_max):
        t = jnp.maximum(t, 0)
        new_max = []
        for h in range(DSA_HEADS):
            r = rt_ref[slot * DSA_HEADS + h]
            a = jnp.exp2(prev_max[h] - r)
            x = jnp.concatenate([vt_ref[h * DSA_DH:(h + 1) * DSA_DH, tile_rows(t)], ones_rows], axis=0)
            acc_ref[acc_rows(h), :] = acc_ref[acc_rows(h), :] * a + _dot(x, p_refs[slot][h])
            new_max.append(r)
        return tuple(new_max)

    acc_ref[...] = jnp.zeros_like(acc_ref)
    rt_ref[...] = jnp.full(rt_ref.shape, NEG_BIG, f32)
    stage_logits(True, 0, 0)
    stage_logits(True, 1, 1)

    def step(near, j, carry):
        probs_max, pv_max = carry
        pv_max = stage_pv(2 * j - 2, 0, pv_max)
        pv_max = stage_pv(2 * j - 1, 1, pv_max)
        probs_max = stage_probs(0, probs_max)
        probs_max = stage_probs(1, probs_max)
        stage_logits(near, 2 * j + 2, 0)
        stage_logits(near, 2 * j + 3, 1)
        return probs_max, pv_max

    neg_rows = tuple(jnp.full((1, QB), NEG_BIG, f32) for _ in range(DSA_HEADS))
    n_far_steps = jnp.maximum((qi - 3) // 2, 0)
    carry = lax.fori_loop(0, n_far_steps, functools.partial(step, False), (neg_rows, neg_rows))
    _, pv_max = lax.fori_loop(n_far_steps, n_pairs, functools.partial(step, True), carry)
    pv_max = stage_pv(2 * n_pairs - 2, 0, pv_max)
    stage_pv(2 * n_pairs - 1, 1, pv_max)
    outs = []
    for h in range(DSA_HEADS):
        acc = acc_ref[acc_rows(h), :]
        outs.append(acc[0:DSA_DH, :] * (1.0 / acc[DSA_DH:DSA_DH + 1, :]))
    o_ref[...] = jnp.concatenate(outs, axis=0).T.astype(o_ref.dtype)


def _dsa(dqk, vt, iq, ikk, iw, bias_tiles, B, S):
    T = dqk.shape[0]
    QB = DSA_QB
    nq = S // QB
    assert nq % 2 == 0
    topk = min(TOPK_MAX, S // 4)
    pos_bits = max(1, int(math.ceil(math.log2(S))))
    kern = functools.partial(_dsa_kernel, topk, pos_bits)
    return pl.pallas_call(
        kern,
        grid=(B, nq),
        in_specs=[
            pl.BlockSpec((QB, DSA_W), lambda b, i: (b * nq + i, 0)),
            pl.BlockSpec((S, DSA_W), lambda b, i: (b, 1), pipeline_mode=pl.Buffered(1)),
            pl.BlockSpec((None, DSA_W, S), lambda b, i: (b, 0, 0), pipeline_mode=pl.Buffered(1)),
            pl.BlockSpec((QB, IDX_Q), lambda b, i: (b * nq + i, 0)),
            pl.BlockSpec((S, LANES), lambda b, i: (b, 0), pipeline_mode=pl.Buffered(1)),
            pl.BlockSpec((QB, LANES), lambda b, i: (b * nq + i, 0)),
            _const_spec(bias_tiles.shape),
        ],
        out_specs=pl.BlockSpec((QB, DSA_W), lambda b, i: (b * nq + i, 0)),
        out_shape=jax.ShapeDtypeStruct((T, DSA_W), bf16),
        scratch_shapes=[pltpu.VMEM((S, QB), jnp.int32),
                        pltpu.VMEM((S, QB), jnp.int16),
                        pltpu.VMEM((S, QB), jnp.int16),
                        pltpu.VMEM((S, QB), f32),
                        pltpu.VMEM((DSA_HEADS * DSA_ACC_ROWS, QB), f32),
                        pltpu.VMEM((IDX_HEADS, LANES, QB), bf16),
                        pltpu.VMEM((DSA_HEADS, LANES, QB), bf16),
                        pltpu.VMEM((DSA_HEADS, QB, QB), f32),
                        pltpu.VMEM((DSA_HEADS, QB, QB), f32),
                        pltpu.VMEM((DSA_HEADS, QB, QB), bf16),
                        pltpu.VMEM((DSA_HEADS, QB, QB), bf16),
                        pltpu.VMEM((2 * DSA_HEADS, 1, QB), f32),
                        pltpu.VMEM((2 * DSA_HEADS, 1, QB), f32)],
        compiler_params=pltpu.CompilerParams(dimension_semantics=("arbitrary", "arbitrary"),
                                             vmem_limit_bytes=VMEM_LIMIT_BYTES),
        name="dsa",
    )(dqk, dqk, vt, iq, ikk, iw, bias_tiles)


def _rel_bucket(dist):
    max_exact = REL_BUCKETS // 2
    d = jnp.maximum(dist, 1).astype(f32)
    large = max_exact + (jnp.log(d / max_exact) / math.log(REL_MAX_DIST / max_exact)
                         * (REL_BUCKETS - max_exact)).astype(jnp.int32)
    large = jnp.minimum(large, REL_BUCKETS - 1)
    return jnp.where(dist < max_exact, dist, large)


def _bias_tiles(rel_bias):
    QB = DSA_QB
    j = jnp.arange(QB, dtype=jnp.int32)[:, None]
    i = jnp.arange(QB, dtype=jnp.int32)[None, :]
    far = rel_bias[REL_BUCKETS - 1].astype(f32)
    tiles = []
    assert 2 * QB - (QB - 1) >= REL_MAX_DIST
    for rel in range(3):
        dist = jnp.maximum(rel * QB + i - j, 0)
        onehot = (_rel_bucket(dist)[None] == jnp.arange(REL_BUCKETS, dtype=jnp.int32)[:, None, None])
        tiles.append(jnp.einsum('bh,bji->hji', (rel_bias.astype(f32) - far[None, :]) * LOG2E,
                                onehot.astype(f32), precision=lax.Precision.HIGHEST))
    return jnp.concatenate(tiles, axis=0)


def _mix_kernel(h_ref, oa_ref, ob_ref, gate_ref, wa_ref, wb_ref, wo_ref, gain_ref, out_ref):
    D = h_ref.shape[1]
    ya = _dot(oa_ref[...], wa_ref[...])
    yb = _dot(ob_ref[...], wb_ref[...])
    ga = jax.nn.sigmoid(gate_ref[:, 0:D].astype(f32))
    gb = jax.nn.sigmoid(gate_ref[:, D:2 * D].astype(f32))
    mixed = (ga * ya + gb * yb).astype(bf16)
    m = _dot(mixed, wo_ref[...])
    out_ref[...] = h_ref[...] + _rms(m, gain_ref[...])


def _mix(h2d, oa, ob, gates, wa, wb, wo, gain):
    T, D = h2d.shape
    tm = min(TOKEN_TILE, T)
    tok = lambda n: pl.BlockSpec((tm, n), lambda t: (t, 0))
    return pl.pallas_call(
        _mix_kernel,
        grid=(T // tm,),
        in_specs=[tok(D), tok(GLA_V), tok(DSA_W), tok(2 * D),
                  _const_spec(wa.shape), _const_spec(wb.shape), _const_spec(wo.shape),
                  _const_spec((1, D))],
        out_specs=tok(D),
        out_shape=jax.ShapeDtypeStruct((T, D), f32),
        compiler_params=pltpu.CompilerParams(dimension_semantics=("arbitrary",),
                                             vmem_limit_bytes=VMEM_LIMIT_BYTES),
        name="mix",
    )(h2d, oa, ob, gates, wa, wb, wo, gain)


def _mlp_kernel(h_ref, p_ref, w1_ref, w2_ref, wple_ref, wg_ref, g_pre_ref, g_post_ref, g_ple_ref,
                out_ref, acc_ref):
    h1 = h_ref[...]
    um = _rms(h1, g_pre_ref[...]).astype(bf16)
    d_ff = w1_ref.shape[1]
    for c in range(d_ff // FF_CHUNK):
        cs = slice(c * FF_CHUNK, (c + 1) * FF_CHUNK)
        hid = jnp.maximum(_dot(um, w1_ref[:, cs]), 0.0)
        part = _dot((hid * hid).astype(bf16), w2_ref[cs, :])
        if c == 0:
            acc_ref[...] = part
        else:
            acc_ref[...] += part
    h2 = h1 + _rms(acc_ref[...], g_post_ref[...])
    e = _dot(p_ref[...].astype(bf16), wple_ref[...]) * jax.nn.sigmoid(_dot(h2.astype(bf16), wg_ref[...]))
    out_ref[...] = h2 + _rms(e, g_ple_ref[...])


def _mlp(h2d, p2d, w1, w2, wple, wg, g_pre, g_post, g_ple):
    T, D = h2d.shape
    tm = min(TOKEN_TILE, T)
    tok = lambda n: pl.BlockSpec((tm, n), lambda t: (t, 0))
    return pl.pallas_call(
        _mlp_kernel,
        grid=(T // tm,),
        in_specs=[tok(D), tok(p2d.shape[1]),
                  _const_spec(w1.shape), _const_spec(w2.shape), _const_spec(wple.shape),
                  _const_spec(wg.shape), _const_spec((1, D)), _const_spec((1, D)), _const_spec((1, D))],
        out_specs=tok(D),
        out_shape=jax.ShapeDtypeStruct((T, D), f32),
        scratch_shapes=[pltpu.VMEM((tm, D), f32)],
        compiler_params=pltpu.CompilerParams(dimension_semantics=("arbitrary",),
                                             vmem_limit_bytes=VMEM_LIMIT_BYTES),
        name="mlp",
    )(h2d, p2d, w1, w2, wple, wg, g_pre, g_post, g_ple)


def _regroup_w_in(w, d_model):
    sizes = (GLA_QK, GLA_QK, GLA_V, GLA_V, GLA_GATE_RANK, DSA_W, DSA_W, DSA_W,
             IDX_Q, IDX_DIM, IDX_HEADS, d_model, d_model)
    pts = np.cumsum(sizes)[:-1].tolist()
    gq, gk, gv, gg, glr, dq, dk, dv, iq, ik, iw, ga, gb = jnp.split(w, pts, axis=1)
    pad = lambda a: jnp.pad(a, ((0, 0), (0, LANES - a.shape[1])))
    w_all = jnp.concatenate(
        [gq * (GLA_DK ** -0.5), gk, gv, gg, dq * (DSA_DH ** -0.5 * LOG2E), dk, iq, ik, ik, ga, gb,
         pad(glr), pad(iw)], axis=1).astype(bf16)
    assert w_all.shape[1] == _P_TOTAL
    return w_all, dv.T.astype(bf16)


def kernel(x, p, rel_bias, ln_mix_pre, ln_mix_post, ln_mlp_pre, ln_mlp_post, ln_ple_post, w_in,
           gla_gate_w2, gla_gate_b, gla_norm, w_branch_a, w_branch_b, w_out, w_mlp_in, w_mlp_out,
           w_ple, w_ple_gate):
    B, S, D = x.shape
    depth = w_in.shape[0]
    T = B * S
    assert S % DSA_QB == 0 and S % GLA_STEP == 0 and T % TOKEN_TILE == 0
    bias_tiles = _bias_tiles(rel_bias)
    h = x.reshape(T, D)
    row = lambda a: a.reshape(1, -1).astype(f32)
    for i in range(depth):
        w_all, w_vt = _regroup_w_in(w_in[i], D)
        w2p = jnp.pad(gla_gate_w2[i], ((0, LANES - GLA_GATE_RANK), (0, 0))).astype(bf16)
        gla_in, glog, dqk, vt, iq, ikk, iw, gates = _proj(
            h, row(ln_mix_pre[i]), w_all, w_vt, w2p, row(gla_gate_b[i]), B, S)
        o_a = _gla(gla_in, glog, row(gla_norm[i]), B, S)
        o_b = _dsa(dqk, vt, iq, ikk, iw, bias_tiles, B, S)
        h = _mix(h, o_a, o_b, gates, w_branch_a[i].astype(bf16), w_branch_b[i].astype(bf16),
                 w_out[i].astype(bf16), row(ln_mix_post[i]))
        h = _mlp(h, p[i].reshape(T, -1), w_mlp_in[i].astype(bf16), w_mlp_out[i].astype(bf16),
                 w_ple[i].astype(bf16), w_ple_gate[i].astype(bf16),
                 row(ln_mlp_pre[i]), row(ln_mlp_post[i]), row(ln_ple_post[i]))
    return h.reshape(B, S, D)
```

```python
import functools
import math

import numpy as np
import jax
import jax.numpy as jnp
from jax import lax
from jax.experimental import pallas as pl
from jax.experimental.pallas import tpu as pltpu

GLA_HEADS = 4
GLA_DK = 64
GLA_DV = 128
GLA_GATE_RANK = 16
GLA_GATE_NORM = 16.0
GLA_CHUNK = 64
DSA_HEADS = 8
DSA_DH = 64
IDX_HEADS = 8
IDX_DIM = 64
TOPK_MAX = 256
REL_BUCKETS = 32
REL_MAX_DIST = 128
EPS = 1e-6

GLA_QK = GLA_HEADS * GLA_DK
GLA_V = GLA_HEADS * GLA_DV
DSA_W = DSA_HEADS * DSA_DH
IDX_Q = IDX_HEADS * IDX_DIM

LANES = 128
SUBLANES = 8
VMEM_LIMIT_BYTES = 56 * 1024 * 1024

TOKEN_TILE = 1024
GLA_STEP = 256
DSA_QB = 256
DSA_ACC_ROWS = DSA_DH + 2 * SUBLANES
FF_CHUNK = 512

LOG2E = math.log2(math.e)
INT_MIN = np.int32(-2 ** 31)
NEG_BIG = -1e30

bf16 = jnp.bfloat16
f32 = jnp.float32


def _const_spec(shape):
    nd = len(shape)
    return pl.BlockSpec(shape, lambda *_: (0,) * nd, pipeline_mode=pl.Buffered(1))


def _rms(xf, gain):
    return xf * lax.rsqrt(jnp.mean(xf * xf, axis=-1, keepdims=True) + EPS) * gain


def _dot(a, b):
    return jnp.dot(a, b, preferred_element_type=f32)


def _dot_nt(a, b):
    return lax.dot_general(a, b, (((1,), (1,)), ((), ())), preferred_element_type=f32)


def _dot_tn(a, b):
    return lax.dot_general(a, b, (((0,), (0,)), ((), ())), preferred_element_type=f32)


_P_GLA = (0, 1536)
_P_DQK = (1536, 2560)
_P_IQ = (2560, 3072)
_P_IKK = (3072, 3200)
_P_GATE = (3200, 5248)
_P_GLR = (5248, 5376)
_P_IW = (5376, 5504)
_P_TOTAL = 5504


def _proj_kernel(h_ref, gain_ref, w_ref, wvt_ref, w2_ref, gb_ref,
                 gla_ref, glog_ref, dqk_ref, vt_ref, iq_ref, ikk_ref, iw_ref, gate_ref):
    u = _rms(h_ref[...], gain_ref[...]).astype(bf16)

    def proj(span):
        return _dot(u, w_ref[:, span[0]:span[1]])

    gla_ref[...] = proj(_P_GLA).astype(bf16)
    dqk_ref[...] = proj(_P_DQK).astype(bf16)
    iq_ref[...] = proj(_P_IQ).astype(bf16)
    ikk_ref[...] = proj(_P_IKK).astype(bf16)
    gate_ref[...] = proj(_P_GATE).astype(bf16)
    iw_ref[...] = proj(_P_IW)
    vt_ref[...] = _dot_nt(wvt_ref[...], u).astype(bf16)
    glr = proj(_P_GLR).astype(bf16)
    x = _dot(glr, w2_ref[...]) + gb_ref[...]
    ls = jnp.minimum(x, 0.0) - jnp.log(1.0 + jnp.exp(-jnp.abs(x)))
    glog_ref[...] = ls * (1.0 / GLA_GATE_NORM)


def _proj(h2d, gain, w_all, w_vt, w2p, gb, B, S):
    T, D = h2d.shape
    tm = min(TOKEN_TILE, S)
    nS = S // tm
    tok = lambda n: pl.BlockSpec((tm, n), lambda t: (t, 0))
    out_shape = (
        jax.ShapeDtypeStruct((T, 1536), bf16),
        jax.ShapeDtypeStruct((T, GLA_QK), f32),
        jax.ShapeDtypeStruct((T, 2 * DSA_W), bf16),
        jax.ShapeDtypeStruct((B, DSA_W, S), bf16),
        jax.ShapeDtypeStruct((T, IDX_Q), bf16),
        jax.ShapeDtypeStruct((T, LANES), bf16),
        jax.ShapeDtypeStruct((T, LANES), f32),
        jax.ShapeDtypeStruct((T, 2 * D), bf16),
    )
    out_specs = (
        tok(1536), tok(GLA_QK), tok(2 * DSA_W),
        pl.BlockSpec((None, DSA_W, tm), lambda t: (t // nS, 0, t % nS)),
        tok(IDX_Q), tok(LANES), tok(LANES), tok(2 * D),
    )
    return pl.pallas_call(
        _proj_kernel,
        grid=(T // tm,),
        in_specs=[tok(D), _const_spec((1, D)), _const_spec(w_all.shape), _const_spec(w_vt.shape),
                  _const_spec(w2p.shape), _const_spec((1, GLA_QK))],
        out_specs=out_specs,
        out_shape=out_shape,
        compiler_params=pltpu.CompilerParams(dimension_semantics=("arbitrary",),
                                             vmem_limit_bytes=VMEM_LIMIT_BYTES),
        name="proj",
    )(h2d, gain, w_all, w_vt, w2p, gb)


def _gla_kernel(x_ref, glog_ref, gn_ref, o_ref, state_ref):
    @pl.when(pl.program_id(1) == 0)
    def _():
        state_ref[...] = jnp.zeros_like(state_ref)

    C = GLA_CHUNK
    row = lax.broadcasted_iota(jnp.int32, (C, C), 0)
    col = lax.broadcasted_iota(jnp.int32, (C, C), 1)
    causal = col <= row
    tril = causal.astype(f32)
    gn = gn_ref[...]

    for c in range(GLA_STEP // C):
        rows = slice(c * C, (c + 1) * C)
        q = x_ref[rows, 0:GLA_QK].astype(f32)
        k = x_ref[rows, GLA_QK:2 * GLA_QK].astype(f32)
        g = glog_ref[rows, :]
        b = jnp.dot(tril, g, preferred_element_type=f32, precision=lax.Precision.HIGHEST)
        b_last = b[C - 1:C, :]
        anchor = 0.5 * b_last
        q_inter = (q * jnp.exp(b)).astype(bf16)
        q_s = (q * jnp.exp(b - anchor)).astype(bf16)
        k_s = (k * jnp.exp(anchor - b)).astype(bf16)
        k_d = (k * jnp.exp(b_last - b)).astype(bf16)
        decay = jnp.exp(b_last)
        state = state_ref[...]
        state_b = state.astype(bf16)
        new_cols = []
        for h in range(GLA_HEADS):
            hs = slice(h * GLA_DK, (h + 1) * GLA_DK)
            vs = slice(2 * GLA_QK + h * GLA_DV, 2 * GLA_QK + (h + 1) * GLA_DV)
            gs = slice(2 * GLA_QK + GLA_V + h * GLA_DV, 2 * GLA_QK + GLA_V + (h + 1) * GLA_DV)
            v = x_ref[rows, vs]
            attn = jnp.where(causal, _dot_nt(q_s[:, hs], k_s[:, hs]), 0.0).astype(bf16)
            o = _dot(attn, v) + _dot_nt(q_inter[:, hs], state_b[:, hs])
            new_cols.append(_dot_tn(v, k_d[:, hs]))
            gg = x_ref[rows, gs].astype(f32)
            o_ref[rows, h * GLA_DV:(h + 1) * GLA_DV] = (
                _rms(o, gn) * (gg * jax.nn.sigmoid(gg))).astype(o_ref.dtype)
        state_ref[...] = state * decay + jnp.concatenate(new_cols, axis=1)


def _gla(gla_in, glog, gn, B, S):
    T = gla_in.shape[0]
    step = GLA_STEP
    nS = S // step
    return pl.pallas_call(
        _gla_kernel,
        grid=(B, nS),
        in_specs=[pl.BlockSpec((step, 1536), lambda b, s: (b * nS + s, 0)),
                  pl.BlockSpec((step, GLA_QK), lambda b, s: (b * nS + s, 0)),
                  _const_spec((1, GLA_DV))],
        out_specs=pl.BlockSpec((step, GLA_V), lambda b, s: (b * nS + s, 0)),
        out_shape=jax.ShapeDtypeStruct((T, GLA_V), bf16),
        scratch_shapes=[pltpu.VMEM((GLA_DV, GLA_QK), f32)],
        compiler_params=pltpu.CompilerParams(dimension_semantics=("arbitrary", "arbitrary"),
                                             vmem_limit_bytes=VMEM_LIMIT_BYTES),
        name="gla",
    )(gla_in, glog, gn)


def _sortable(x):
    bits = lax.bitcast_convert_type(x + 0.0, jnp.int32)
    return bits ^ ((bits >> 31) & jnp.int32(0x7FFFFFFF))


def _dsa_kernel(topk, pos_bits, q_ref, k_ref, vt_ref, iq_ref, ikk_ref, iw_ref, bias_ref,
                o_ref, keys_ref, hi_ref, lo_ref, madd_ref, acc_ref, wiq_ref, wq_ref,
                l0_ref, l1_ref, p0_ref, p1_ref, ml_ref, rt_ref):
    QB = DSA_QB
    G = QB // SUBLANES
    qi = pl.program_id(1)
    n_tiles = qi + 1
    idx_scale = (IDX_HEADS ** -0.5) * (IDX_DIM ** -0.5)

    row_half = lax.broadcasted_iota(jnp.int32, (LANES, QB), 0) // DSA_DH
    key_j = lax.broadcasted_iota(jnp.int32, (QB, QB), 0)
    qry_i = lax.broadcasted_iota(jnp.int32, (QB, QB), 1)
    tri = key_j <= qry_i

    def stage_head_operands(src_ref, dst_ref):
        t = src_ref[...].astype(f32).T
        for h in range(DSA_HEADS):
            pair = t[(h // 2) * LANES:(h // 2 + 1) * LANES, :]
            dst_ref[h] = jnp.where(row_half == (h % 2), pair, 0.0).astype(bf16)

    def group_sum(x):
        return jnp.sum(x.reshape(G, SUBLANES, QB), axis=0)

    def group_max(x):
        return jnp.max(x.reshape(G, SUBLANES, QB), axis=0)

    def tile_rows(kt):
        return pl.ds(pl.multiple_of(kt * QB, QB), QB)

    n_pairs = (n_tiles + 1) // 2

    def pad_odd_tile(ref):
        @pl.when(n_tiles % 2 == 1)
        def _():
            ref[tile_rows(n_tiles), :] = jnp.full((QB, QB), -2 ** 15, jnp.int16)

    w_t = iw_ref[...].T
    stage_head_operands(iq_ref, wiq_ref)

    def score_tile(kt, carry):
        kk = ikk_ref[tile_rows(kt), :]
        acc = jnp.zeros((QB, QB), f32)
        for h in range(IDX_HEADS):
            acc = acc + jnp.maximum(_dot(kk, wiq_ref[h]), 0.0) * w_t[h:h + 1, :]
        key = _sortable(acc * idx_scale)
        visible = jnp.logical_or(kt < qi, tri)
        key = jnp.where(visible, key, INT_MIN)
        keys_ref[tile_rows(kt), :] = key
        hi_ref[tile_rows(kt), :] = (key >> 16).astype(jnp.int16)
        return carry

    lax.fori_loop(0, n_tiles, score_tile, 0)
    pad_odd_tile(hi_ref)

    def count(pred):
        def body(kt, cnt):
            return cnt + group_sum(jnp.where(pred(keys_ref[tile_rows(kt), :], kt), 1, 0))
        cnt = lax.fori_loop(0, n_tiles, body, jnp.zeros((SUBLANES, QB), jnp.int32))
        return jnp.sum(cnt, axis=0, keepdims=True)

    PACK = 2 * SUBLANES
    one16 = jnp.ones((), jnp.int16)
    zero16 = jnp.zeros((), jnp.int16)

    def count16(ref, pred):
        def body(kp, cnt):
            blk = ref[pl.ds(pl.multiple_of(kp * 2 * QB, 2 * QB), 2 * QB), :]
            c3 = jnp.where(pred(blk), one16, zero16).reshape(2 * QB // PACK, PACK, QB)
            parts = [c3[g] for g in range(4)]
            for g in range(4, 2 * QB // PACK):
                parts[g % 4] = parts[g % 4] + c3[g]
            return cnt + ((parts[0] + parts[1]) + (parts[2] + parts[3]))
        cnt = lax.fori_loop(0, n_pairs, body, jnp.zeros((PACK, QB), jnp.int16))
        return jnp.sum(cnt.astype(jnp.int32), axis=0, keepdims=True)

    zero_row = jnp.zeros((1, QB), jnp.int32)

    def count_hi_ge(trial):
        t16 = trial.astype(jnp.int16)
        return count16(hi_ref, lambda blk: blk >= t16)

    a1 = jnp.where(count_hi_ge(zero_row) >= topk, zero_row, zero_row - 2 ** 15)

    def hi_step(i, a1):
        trial = a1 | (jnp.int32(1) << (14 - i))
        return jnp.where(count_hi_ge(trial) >= topk, trial, a1)

    a1 = lax.fori_loop(0, 15, hi_step, a1)
    a1_16 = a1.astype(jnp.int16)
    k2 = topk - count16(hi_ref, lambda blk: blk > a1_16)

    def low_tile(kt, carry):
        key = keys_ref[tile_rows(kt), :]
        low = jnp.where((key >> 16) == a1, (key & 0xFFFF) - 2 ** 15, -2 ** 15)
        lo_ref[tile_rows(kt), :] = low.astype(jnp.int16)
        return carry

    lax.fori_loop(0, n_tiles, low_tile, 0)
    pad_odd_tile(lo_ref)

    def lo_step(i, a2):
        trial = a2 | (jnp.int32(1) << (15 - i))
        t16 = (trial - 2 ** 15).astype(jnp.int16)
        return jnp.where(count16(lo_ref, lambda blk: blk >= t16) >= k2, trial, a2)

    a2 = lax.fori_loop(0, 16, lo_step, zero_row)
    ans = (a1 << 16) | a2
    thr = jnp.maximum(ans, INT_MIN + 1)

    a2_16 = (a2 - 2 ** 15).astype(jnp.int16)
    c_ge = (topk - k2) + count16(lo_ref, lambda blk: blk >= a2_16)
    tied = jnp.logical_and(c_ge > topk, ans > INT_MIN)
    any_tied = jnp.max(jnp.where(tied, 1, 0)) > 0

    def pos_of(kt):
        return key_j + kt * QB

    def tie_cutoff():
        need = topk - count(lambda blk, kt: blk > thr)

        def pbit(i, cut):
            trial = cut | (jnp.int32(1) << (pos_bits - 1 - i))
            c = count(lambda blk, kt: jnp.logical_and(blk == thr, pos_of(kt) < trial))
            return jnp.where(c < need, trial, cut)

        cut = lax.fori_loop(0, pos_bits, pbit, zero_row)
        return jnp.where(tied, cut, jnp.int32(2 ** 30))

    cut = lax.cond(any_tied, tie_cutoff, lambda: zero_row + jnp.int32(2 ** 30))

    def mask_tile(kt, carry):
        blk = keys_ref[tile_rows(kt), :]
        sel = jnp.logical_or(blk > thr, jnp.logical_and(blk == thr, pos_of(kt) <= cut))
        madd_ref[tile_rows(kt), :] = jnp.where(sel, 0.0, NEG_BIG).astype(f32)
        return carry

    lax.fori_loop(0, n_tiles, mask_tile, 0)

    stage_head_operands(q_ref, wq_ref)

    @pl.when(n_tiles % 2 == 1)
    def _():
        madd_ref[tile_rows(n_tiles), :] = jnp.full((QB, QB), NEG_BIG, f32)

    @pl.when(jnp.logical_and(pl.program_id(0) == 0, qi == 0))
    def _():
        p0_ref[...] = jnp.zeros_like(p0_ref)
        p1_ref[...] = jnp.zeros_like(p1_ref)

    l_refs = (l0_ref, l1_ref)
    p_refs = (p0_ref, p1_ref)
    last_tile = 2 * n_pairs - 1
    acc_rows = lambda h: slice(h * DSA_ACC_ROWS, (h + 1) * DSA_ACC_ROWS)
    ones_rows = jnp.ones((DSA_ACC_ROWS - DSA_DH, QB), bf16)

    def stage_logits(near, t, slot):
        if near:
            t = jnp.minimum(t, last_tile)
            rel = qi - t
            bias_idx = jnp.where(rel < 0, 2, jnp.minimum(rel, 2)) * DSA_HEADS
        for h in range(DSA_HEADS):
            kp = k_ref[tile_rows(t), (h // 2) * LANES:(h // 2 + 1) * LANES]
            l = _dot(kp, wq_ref[h]) + madd_ref[tile_rows(t), :]
            if near:
                l = l + bias_ref[bias_idx + h]
            l_refs[slot][h] = l
            ml_ref[slot * DSA_HEADS + h] = jnp.max(group_max(l), axis=0, keepdims=True)

    def stage_probs(slot, run_max):
        new_max = []
        for h in range(DSA_HEADS):
            r = jnp.maximum(run_max[h], ml_ref[slot * DSA_HEADS + h])
            p_refs[slot][h] = jnp.exp2(l_refs[slot][h] - r).astype(bf16)
            rt_ref[slot * DSA_HEADS + h] = r
            new_max.append(r)
        return tuple(new_max)

    def stage_pv(t, slot, prev_max):
        t = jnp.maximum(t, 0)
        new_max = []
        for h in range(DSA_HEADS):
            r = rt_ref[slot * DSA_HEADS + h]
            a = jnp.exp2(prev_max[h] - r)
            x = jnp.concatenate([vt_ref[h * DSA_DH:(h + 1) * DSA_DH, tile_rows(t)], ones_rows], axis=0)
            acc_ref[acc_rows(h), :] = acc_ref[acc_rows(h), :] * a + _dot(x, p_refs[slot][h])
            new_max.append(r)
        return tuple(new_max)

    acc_ref[...] = jnp.zeros_like(acc_ref)
    rt_ref[...] = jnp.full(rt_ref.shape, NEG_BIG, f32)
    stage_logits(True, 0, 0)
    stage_logits(True, 1, 1)

    def step(near, j, carry):
        probs_max, pv_max = carry
        pv_max = stage_pv(2 * j - 2, 0, pv_max)
        pv_max = stage_pv(2 * j - 1, 1, pv_max)
        probs_max = stage_probs(0, probs_max)
        probs_max = stage_probs(1, probs_max)
        stage_logits(near, 2 * j + 2, 0)
        stage_logits(near, 2 * j + 3, 1)
        return probs_max, pv_max

    neg_rows = tuple(jnp.full((1, QB), NEG_BIG, f32) for _ in range(DSA_HEADS))
    n_far_steps = jnp.maximum((qi - 3) // 2, 0)
    carry = lax.fori_loop(0, n_far_steps, functools.partial(step, False), (neg_rows, neg_rows))
    _, pv_max = lax.fori_loop(n_far_steps, n_pairs, functools.partial(step, True), carry)
    pv_max = stage_pv(2 * n_pairs - 2, 0, pv_max)
    stage_pv(2 * n_pairs - 1, 1, pv_max)
    outs = []
    for h in range(DSA_HEADS):
        acc = acc_ref[acc_rows(h), :]
        outs.append(acc[0:DSA_DH, :] * (1.0 / acc[DSA_DH:DSA_DH + 1, :]))
    o_ref[...] = jnp.concatenate(outs, axis=0).T.astype(o_ref.dtype)


def _dsa(dqk, vt, iq, ikk, iw, bias_tiles, B, S):
    T = dqk.shape[0]
    QB = DSA_QB
    nq = S // QB
    assert nq % 2 == 0
    topk = min(TOPK_MAX, S // 4)
    pos_bits = max(1, int(math.ceil(math.log2(S))))
    kern = functools.partial(_dsa_kernel, topk, pos_bits)
    return pl.pallas_call(
        kern,
        grid=(B, nq),
        in_specs=[
            pl.BlockSpec((QB, DSA_W), lambda b, i: (b * nq + i, 0)),
            pl.BlockSpec((S, DSA_W), lambda b, i: (b, 1), pipeline_mode=pl.Buffered(1)),
            pl.BlockSpec((None, DSA_W, S), lambda b, i: (b, 0, 0), pipeline_mode=pl.Buffered(1)),
            pl.BlockSpec((QB, IDX_Q), lambda b, i: (b * nq + i, 0)),
            pl.BlockSpec((S, LANES), lambda b, i: (b, 0), pipeline_mode=pl.Buffered(1)),
            pl.BlockSpec((QB, LANES), lambda b, i: (b * nq + i, 0)),
            _const_spec(bias_tiles.shape),
        ],
        out_specs=pl.BlockSpec((QB, DSA_W), lambda b, i: (b * nq + i, 0)),
        out_shape=jax.ShapeDtypeStruct((T, DSA_W), bf16),
        scratch_shapes=[pltpu.VMEM((S, QB), jnp.int32),
                        pltpu.VMEM((S, QB), jnp.int16),
                        pltpu.VMEM((S, QB), jnp.int16),
                        pltpu.VMEM((S, QB), f32),
                        pltpu.VMEM((DSA_HEADS * DSA_ACC_ROWS, QB), f32),
                        pltpu.VMEM((IDX_HEADS, LANES, QB), bf16),
                        pltpu.VMEM((DSA_HEADS, LANES, QB), bf16),
                        pltpu.VMEM((DSA_HEADS, QB, QB), f32),
                        pltpu.VMEM((DSA_HEADS, QB, QB), f32),
                        pltpu.VMEM((DSA_HEADS, QB, QB), bf16),
                        pltpu.VMEM((DSA_HEADS, QB, QB), bf16),
                        pltpu.VMEM((2 * DSA_HEADS, 1, QB), f32),
                        pltpu.VMEM((2 * DSA_HEADS, 1, QB), f32)],
        compiler_params=pltpu.CompilerParams(dimension_semantics=("arbitrary", "arbitrary"),
                                             vmem_limit_bytes=VMEM_LIMIT_BYTES),
        name="dsa",
    )(dqk, dqk, vt, iq, ikk, iw, bias_tiles)


def _rel_bucket(dist):
    max_exact = REL_BUCKETS // 2
    d = jnp.maximum(dist, 1).astype(f32)
    large = max_exact + (jnp.log(d / max_exact) / math.log(REL_MAX_DIST / max_exact)
                         * (REL_BUCKETS - max_exact)).astype(jnp.int32)
    large = jnp.minimum(large, REL_BUCKETS - 1)
    return jnp.where(dist < max_exact, dist, large)


def _bias_tiles(rel_bias):
    QB = DSA_QB
    j = jnp.arange(QB, dtype=jnp.int32)[:, None]
    i = jnp.arange(QB, dtype=jnp.int32)[None, :]
    far = rel_bias[REL_BUCKETS - 1].astype(f32)
    tiles = []
    assert 2 * QB - (QB - 1) >= REL_MAX_DIST
    for rel in range(3):
        dist = jnp.maximum(rel * QB + i - j, 0)
        onehot = (_rel_bucket(dist)[None] == jnp.arange(REL_BUCKETS, dtype=jnp.int32)[:, None, None])
        tiles.append(jnp.einsum('bh,bji->hji', (rel_bias.astype(f32) - far[None, :]) * LOG2E,
                                onehot.astype(f32), precision=lax.Precision.HIGHEST))
    return jnp.concatenate(tiles, axis=0)


def _mix_kernel(h_ref, oa_ref, ob_ref, gate_ref, wa_ref, wb_ref, wo_ref, gain_ref, out_ref):
    D = h_ref.shape[1]
    ya = _dot(oa_ref[...], wa_ref[...])
    yb = _dot(ob_ref[...], wb_ref[...])
    ga = jax.nn.sigmoid(gate_ref[:, 0:D].astype(f32))
    gb = jax.nn.sigmoid(gate_ref[:, D:2 * D].astype(f32))
    mixed = (ga * ya + gb * yb).astype(bf16)
    m = _dot(mixed, wo_ref[...])
    out_ref[...] = h_ref[...] + _rms(m, gain_ref[...])


def _mix(h2d, oa, ob, gates, wa, wb, wo, gain):
    T, D = h2d.shape
    tm = min(TOKEN_TILE, T)
    tok = lambda n: pl.BlockSpec((tm, n), lambda t: (t, 0))
    return pl.pallas_call(
        _mix_kernel,
        grid=(T // tm,),
        in_specs=[tok(D), tok(GLA_V), tok(DSA_W), tok(2 * D),
                  _const_spec(wa.shape), _const_spec(wb.shape), _const_spec(wo.shape),
                  _const_spec((1, D))],
        out_specs=tok(D),
        out_shape=jax.ShapeDtypeStruct((T, D), f32),
        compiler_params=pltpu.CompilerParams(dimension_semantics=("arbitrary",),
                                             vmem_limit_bytes=VMEM_LIMIT_BYTES),
        name="mix",
    )(h2d, oa, ob, gates, wa, wb, wo, gain)


def _mlp_kernel(h_ref, p_ref, w1_ref, w2_ref, wple_ref, wg_ref, g_pre_ref, g_post_ref, g_ple_ref,
                out_ref, acc_ref):
    h1 = h_ref[...]
    um = _rms(h1, g_pre_ref[...]).astype(bf16)
    d_ff = w1_ref.shape[1]
    for c in range(d_ff // FF_CHUNK):
        cs = slice(c * FF_CHUNK, (c + 1) * FF_CHUNK)
        hid = jnp.maximum(_dot(um, w1_ref[:, cs]), 0.0)
        part = _dot((hid * hid).astype(bf16), w2_ref[cs, :])
        if c == 0:
            acc_ref[...] = part
        else:
            acc_ref[...] += part
    h2 = h1 + _rms(acc_ref[...], g_post_ref[...])
    e = _dot(p_ref[...].astype(bf16), wple_ref[...]) * jax.nn.sigmoid(_dot(h2.astype(bf16), wg_ref[...]))
    out_ref[...] = h2 + _rms(e, g_ple_ref[...])


def _mlp(h2d, p2d, w1, w2, wple, wg, g_pre, g_post, g_ple):
    T, D = h2d.shape
    tm = min(TOKEN_TILE, T)
    tok = lambda n: pl.BlockSpec((tm, n), lambda t: (t, 0))
    return pl.pallas_call(
        _mlp_kernel,
        grid=(T // tm,),
        in_specs=[tok(D), tok(p2d.shape[1]),
                  _const_spec(w1.shape), _const_spec(w2.shape), _const_spec(wple.shape),
                  _const_spec(wg.shape), _const_spec((1, D)), _const_spec((1, D)), _const_spec((1, D))],
        out_specs=tok(D),
        out_shape=jax.ShapeDtypeStruct((T, D), f32),
        scratch_shapes=[pltpu.VMEM((tm, D), f32)],
        compiler_params=pltpu.CompilerParams(dimension_semantics=("arbitrary",),
                                             vmem_limit_bytes=VMEM_LIMIT_BYTES),
        name="mlp",
    )(h2d, p2d, w1, w2, wple, wg, g_pre, g_post, g_ple)


def _regroup_w_in(w, d_model):
    sizes = (GLA_QK, GLA_QK, GLA_V, GLA_V, GLA_GATE_RANK, DSA_W, DSA_W, DSA_W,
             IDX_Q, IDX_DIM, IDX_HEADS, d_model, d_model)
    pts = np.cumsum(sizes)[:-1].tolist()
    gq, gk, gv, gg, glr, dq, dk, dv, iq, ik, iw, ga, gb = jnp.split(w, pts, axis=1)
    pad = lambda a: jnp.pad(a, ((0, 0), (0, LANES - a.shape[1])))
    w_all = jnp.concatenate(
        [gq * (GLA_DK ** -0.5), gk, gv, gg, dq * (DSA_DH ** -0.5 * LOG2E), dk, iq, ik, ik, ga, gb,
         pad(glr), pad(iw)], axis=1).astype(bf16)
    assert w_all.shape[1] == _P_TOTAL
    return w_all, dv.T.astype(bf16)


def kernel(x, p, rel_bias, ln_mix_pre, ln_mix_post, ln_mlp_pre, ln_mlp_post, ln_ple_post, w_in,
           gla_gate_w2, gla_gate_b, gla_norm, w_branch_a, w_branch_b, w_out, w_mlp_in, w_mlp_out,
           w_ple, w_ple_gate):
    B, S, D = x.shape
    depth = w_in.shape[0]
    T = B * S
    assert S % DSA_QB == 0 and S % GLA_STEP == 0 and T % TOKEN_TILE == 0
    bias_tiles = _bias_tiles(rel_bias)
    h = x.reshape(T, D)
    row = lambda a: a.reshape(1, -1).astype(f32)
    for i in range(depth):
        w_all, w_vt = _regroup_w_in(w_in[i], D)
        w2p = jnp.pad(gla_gate_w2[i], ((0, LANES - GLA_GATE_RANK), (0, 0))).astype(bf16)
        gla_in, glog, dqk, vt, iq, ikk, iw, gates = _proj(
            h, row(ln_mix_pre[i]), w_all, w_vt, w2p, row(gla_gate_b[i]), B, S)
        o_a = _gla(gla_in, glog, row(gla_norm[i]), B, S)
        o_b = _dsa(dqk, vt, iq, ikk, iw, bias_tiles, B, S)
        h = _mix(h, o_a, o_b, gates, w_branch_a[i].astype(bf16), w_branch_b[i].astype(bf16),
                 w_out[i].astype(bf16), row(ln_mix_post[i]))
        h = _mlp(h, p[i].reshape(T, -1), w_mlp_in[i].astype(bf16), w_mlp_out[i].astype(bf16),
                 w_ple[i].astype(bf16), w_ple_gate[i].astype(bf16),
                 row(ln_mlp_pre[i]), row(ln_mlp_post[i]), row(ln_ple_post[i]))
    return h.reshape(B, S, D)
```

```python
import functools
import math

import numpy as np
import jax
import jax.numpy as jnp
from jax import lax
from jax.experimental import pallas as pl
from jax.experimental.pallas import tpu as pltpu

GLA_HEADS = 4
GLA_DK = 64
GLA_DV = 128
GLA_GATE_RANK = 16
GLA_GATE_NORM = 16.0
GLA_CHUNK = 64
DSA_HEADS = 8
DSA_DH = 64
IDX_HEADS = 8
IDX_DIM = 64
TOPK_MAX = 256
REL_BUCKETS = 32
REL_MAX_DIST = 128
EPS = 1e-6

GLA_QK = GLA_HEADS * GLA_DK
GLA_V = GLA_HEADS * GLA_DV
DSA_W = DSA_HEADS * DSA_DH
IDX_Q = IDX_HEADS * IDX_DIM

LANES = 128
SUBLANES = 8
VMEM_LIMIT_BYTES = 56 * 1024 * 1024

TOKEN_TILE = 1024
GLA_STEP = 256
DSA_QB = 256
DSA_ACC_ROWS = DSA_DH + 2 * SUBLANES
FF_CHUNK = 512

LOG2E = math.log2(math.e)
INT_MIN = np.int32(-2 ** 31)
NEG_BIG = -1e30

bf16 = jnp.bfloat16
f32 = jnp.float32


def _const_spec(shape):
    nd = len(shape)
    return pl.BlockSpec(shape, lambda *_: (0,) * nd, pipeline_mode=pl.Buffered(1))


def _rms(xf, gain):
    return xf * lax.rsqrt(jnp.mean(xf * xf, axis=-1, keepdims=True) + EPS) * gain


def _dot(a, b):
    return jnp.dot(a, b, preferred_element_type=f32)


def _dot_nt(a, b):
    return lax.dot_general(a, b, (((1,), (1,)), ((), ())), preferred_element_type=f32)


def _dot_tn(a, b):
    return lax.dot_general(a, b, (((0,), (0,)), ((), ())), preferred_element_type=f32)


_P_GLA = (0, 1536)
_P_DQK = (1536, 2560)
_P_IQ = (2560, 3072)
_P_IKK = (3072, 3200)
_P_GATE = (3200, 5248)
_P_GLR = (5248, 5376)
_P_IW = (5376, 5504)
_P_TOTAL = 5504


def _proj_kernel(h_ref, gain_ref, w_ref, wvt_ref, w2_ref, gb_ref,
                 gla_ref, glog_ref, dqk_ref, vt_ref, iq_ref, ikk_ref, iw_ref, gate_ref):
    u = _rms(h_ref[...], gain_ref[...]).astype(bf16)

    def proj(span):
        return _dot(u, w_ref[:, span[0]:span[1]])

    gla_ref[...] = proj(_P_GLA).astype(bf16)
    dqk_ref[...] = proj(_P_DQK).astype(bf16)
    iq_ref[...] = proj(_P_IQ).astype(bf16)
    ikk_ref[...] = proj(_P_IKK).astype(bf16)
    gate_ref[...] = proj(_P_GATE).astype(bf16)
    iw_ref[...] = proj(_P_IW)
    vt_ref[...] = _dot_nt(wvt_ref[...], u).astype(bf16)
    glr = proj(_P_GLR).astype(bf16)
    x = _dot(glr, w2_ref[...]) + gb_ref[...]
    ls = jnp.minimum(x, 0.0) - jnp.log(1.0 + jnp.exp(-jnp.abs(x)))
    glog_ref[...] = ls * (1.0 / GLA_GATE_NORM)


def _proj(h2d, gain, w_all, w_vt, w2p, gb, B, S):
    T, D = h2d.shape
    tm = min(TOKEN_TILE, S)
    nS = S // tm
    tok = lambda n: pl.BlockSpec((tm, n), lambda t: (t, 0))
    out_shape = (
        jax.ShapeDtypeStruct((T, 1536), bf16),
        jax.ShapeDtypeStruct((T, GLA_QK), f32),
        jax.ShapeDtypeStruct((T, 2 * DSA_W), bf16),
        jax.ShapeDtypeStruct((B, DSA_W, S), bf16),
        jax.ShapeDtypeStruct((T, IDX_Q), bf16),
        jax.ShapeDtypeStruct((T, LANES), bf16),
        jax.ShapeDtypeStruct((T, LANES), f32),
        jax.ShapeDtypeStruct((T, 2 * D), bf16),
    )
    out_specs = (
        tok(1536), tok(GLA_QK), tok(2 * DSA_W),
        pl.BlockSpec((None, DSA_W, tm), lambda t: (t // nS, 0, t % nS)),
        tok(IDX_Q), tok(LANES), tok(LANES), tok(2 * D),
    )
    return pl.pallas_call(
        _proj_kernel,
        grid=(T // tm,),
        in_specs=[tok(D), _const_spec((1, D)), _const_spec(w_all.shape), _const_spec(w_vt.shape),
                  _const_spec(w2p.shape), _const_spec((1, GLA_QK))],
        out_specs=out_specs,
        out_shape=out_shape,
        compiler_params=pltpu.CompilerParams(dimension_semantics=("arbitrary",),
                                             vmem_limit_bytes=VMEM_LIMIT_BYTES),
        name="proj",
    )(h2d, gain, w_all, w_vt, w2p, gb)


def _gla_kernel(x_ref, glog_ref, gn_ref, o_ref, state_ref):
    @pl.when(pl.program_id(1) == 0)
    def _():
        state_ref[...] = jnp.zeros_like(state_ref)

    C = GLA_CHUNK
    row = lax.broadcasted_iota(jnp.int32, (C, C), 0)
    col = lax.broadcasted_iota(jnp.int32, (C, C), 1)
    causal = col <= row
    tril = causal.astype(f32)
    gn = gn_ref[...]

    for c in range(GLA_STEP // C):
        rows = slice(c * C, (c + 1) * C)
        q = x_ref[rows, 0:GLA_QK].astype(f32)
        k = x_ref[rows, GLA_QK:2 * GLA_QK].astype(f32)
        g = glog_ref[rows, :]
        b = jnp.dot(tril, g, preferred_element_type=f32, precision=lax.Precision.HIGHEST)
        b_last = b[C - 1:C, :]
        anchor = 0.5 * b_last
        q_inter = (q * jnp.exp(b)).astype(bf16)
        q_s = (q * jnp.exp(b - anchor)).astype(bf16)
        k_s = (k * jnp.exp(anchor - b)).astype(bf16)
        k_d = (k * jnp.exp(b_last - b)).astype(bf16)
        decay = jnp.exp(b_last)
        state = state_ref[...]
        state_b = state.astype(bf16)
        new_cols = []
        for h in range(GLA_HEADS):
            hs = slice(h * GLA_DK, (h + 1) * GLA_DK)
            vs = slice(2 * GLA_QK + h * GLA_DV, 2 * GLA_QK + (h + 1) * GLA_DV)
            gs = slice(2 * GLA_QK + GLA_V + h * GLA_DV, 2 * GLA_QK + GLA_V + (h + 1) * GLA_DV)
            v = x_ref[rows, vs]
            attn = jnp.where(causal, _dot_nt(q_s[:, hs], k_s[:, hs]), 0.0).astype(bf16)
            o = _dot(attn, v) + _dot_nt(q_inter[:, hs], state_b[:, hs])
            new_cols.append(_dot_tn(v, k_d[:, hs]))
            gg = x_ref[rows, gs].astype(f32)
            o_ref[rows, h * GLA_DV:(h + 1) * GLA_DV] = (
                _rms(o, gn) * (gg * jax.nn.sigmoid(gg))).astype(o_ref.dtype)
        state_ref[...] = state * decay + jnp.concatenate(new_cols, axis=1)


def _gla(gla_in, glog, gn, B, S):
    T = gla_in.shape[0]
    step = GLA_STEP
    nS = S // step
    return pl.pallas_call(
        _gla_kernel,
        grid=(B, nS),
        in_specs=[pl.BlockSpec((step, 1536), lambda b, s: (b * nS + s, 0)),
                  pl.BlockSpec((step, GLA_QK), lambda b, s: (b * nS + s, 0)),
                  _const_spec((1, GLA_DV))],
        out_specs=pl.BlockSpec((step, GLA_V), lambda b, s: (b * nS + s, 0)),
        out_shape=jax.ShapeDtypeStruct((T, GLA_V), bf16),
        scratch_shapes=[pltpu.VMEM((GLA_DV, GLA_QK), f32)],
        compiler_params=pltpu.CompilerParams(dimension_semantics=("arbitrary", "arbitrary"),
                                             vmem_limit_bytes=VMEM_LIMIT_BYTES),
        name="gla",
    )(gla_in, glog, gn)


def _sortable(x):
    bits = lax.bitcast_convert_type(x + 0.0, jnp.int32)
    return bits ^ ((bits >> 31) & jnp.int32(0x7FFFFFFF))


def _dsa_kernel(topk, pos_bits, q_ref, k_ref, vt_ref, iq_ref, ikk_ref, iw_ref, bias_ref,
                o_ref, keys_ref, hi_ref, lo_ref, madd_ref, acc_ref, wiq_ref, wq_ref,
                l0_ref, l1_ref, p0_ref, p1_ref, ml_ref, rt_ref):
    QB = DSA_QB
    G = QB // SUBLANES
    qi = pl.program_id(1)
    n_tiles = qi + 1
    idx_scale = (IDX_HEADS ** -0.5) * (IDX_DIM ** -0.5)

    row_half = lax.broadcasted_iota(jnp.int32, (LANES, QB), 0) // DSA_DH
    key_j = lax.broadcasted_iota(jnp.int32, (QB, QB), 0)
    qry_i = lax.broadcasted_iota(jnp.int32, (QB, QB), 1)
    tri = key_j <= qry_i

    def stage_head_operands(src_ref, dst_ref):
        t = src_ref[...].astype(f32).T
        for h in range(DSA_HEADS):
            pair = t[(h // 2) * LANES:(h // 2 + 1) * LANES, :]
            dst_ref[h] = jnp.where(row_half == (h % 2), pair, 0.0).astype(bf16)

    def group_sum(x):
        return jnp.sum(x.reshape(G, SUBLANES, QB), axis=0)

    def group_max(x):
        return jnp.max(x.reshape(G, SUBLANES, QB), axis=0)

    def tile_rows(kt):
        return pl.ds(pl.multiple_of(kt * QB, QB), QB)

    n_pairs = (n_tiles + 1) // 2

    def pad_odd_tile(ref):
        @pl.when(n_tiles % 2 == 1)
        def _():
            ref[tile_rows(n_tiles), :] = jnp.full((QB, QB), -2 ** 15, jnp.int16)

    w_t = iw_ref[...].T
    stage_head_operands(iq_ref, wiq_ref)

    def score_tile(kt, carry):
        kk = ikk_ref[tile_rows(kt), :]
        acc = jnp.zeros((QB, QB), f32)
        for h in range(IDX_HEADS):
            acc = acc + jnp.maximum(_dot(kk, wiq_ref[h]), 0.0) * w_t[h:h + 1, :]
        key = _sortable(acc * idx_scale)
        visible = jnp.logical_or(kt < qi, tri)
        key = jnp.where(visible, key, INT_MIN)
        keys_ref[tile_rows(kt), :] = key
        hi_ref[tile_rows(kt), :] = (key >> 16).astype(jnp.int16)
        return carry

    lax.fori_loop(0, n_tiles, score_tile, 0)
    pad_odd_tile(hi_ref)

    def count(pred):
        def body(kt, cnt):
            return cnt + group_sum(jnp.where(pred(keys_ref[tile_rows(kt), :], kt), 1, 0))
        cnt = lax.fori_loop(0, n_tiles, body, jnp.zeros((SUBLANES, QB), jnp.int32))
        return jnp.sum(cnt, axis=0, keepdims=True)

    PACK = 2 * SUBLANES
    one16 = jnp.ones((), jnp.int16)
    zero16 = jnp.zeros((), jnp.int16)

    def count16(ref, pred):
        def body(kp, cnt):
            blk = ref[pl.ds(pl.multiple_of(kp * 2 * QB, 2 * QB), 2 * QB), :]
            c3 = jnp.where(pred(blk), one16, zero16).reshape(2 * QB // PACK, PACK, QB)
            parts = [c3[g] for g in range(4)]
            for g in range(4, 2 * QB // PACK):
                parts[g % 4] = parts[g % 4] + c3[g]
            return cnt + ((parts[0] + parts[1]) + (parts[2] + parts[3]))
        cnt = lax.fori_loop(0, n_pairs, body, jnp.zeros((PACK, QB), jnp.int16))
        return jnp.sum(cnt.astype(jnp.int32), axis=0, keepdims=True)

    zero_row = jnp.zeros((1, QB), jnp.int32)

    def count_hi_ge(trial):
        t16 = trial.astype(jnp.int16)
        return count16(hi_ref, lambda blk: blk >= t16)

    a1 = jnp.where(count_hi_ge(zero_row) >= topk, zero_row, zero_row - 2 ** 15)

    def hi_step(i, a1):
        trial = a1 | (jnp.int32(1) << (14 - i))
        return jnp.where(count_hi_ge(trial) >= topk, trial, a1)

    a1 = lax.fori_loop(0, 15, hi_step, a1)
    a1_16 = a1.astype(jnp.int16)
    k2 = topk - count16(hi_ref, lambda blk: blk > a1_16)

    def low_tile(kt, carry):
        key = keys_ref[tile_rows(kt), :]
        low = jnp.where((key >> 16) == a1, (key & 0xFFFF) - 2 ** 15, -2 ** 15)
        lo_ref[tile_rows(kt), :] = low.astype(jnp.int16)
        return carry

    lax.fori_loop(0, n_tiles, low_tile, 0)
    pad_odd_tile(lo_ref)

    def lo_step(i, a2):
        trial = a2 | (jnp.int32(1) << (15 - i))
        t16 = (trial - 2 ** 15).astype(jnp.int16)
        return jnp.where(count16(lo_ref, lambda blk: blk >= t16) >= k2, trial, a2)

    a2 = lax.fori_loop(0, 16, lo_step, zero_row)
    ans = (a1 << 16) | a2
    thr = jnp.maximum(ans, INT_MIN + 1)

    a2_16 = (a2 - 2 ** 15).astype(jnp.int16)
    c_ge = (topk - k2) + count16(lo_ref, lambda blk: blk >= a2_16)
    tied = jnp.logical_and(c_ge > topk, ans > INT_MIN)
    any_tied = jnp.max(jnp.where(tied, 1, 0)) > 0

    def pos_of(kt):
        return key_j + kt * QB

    def tie_cutoff():
        need = topk - count(lambda blk, kt: blk > thr)

        def pbit(i, cut):
            trial = cut | (jnp.int32(1) << (pos_bits - 1 - i))
            c = count(lambda blk, kt: jnp.logical_and(blk == thr, pos_of(kt) < trial))
            return jnp.where(c < need, trial, cut)

        cut = lax.fori_loop(0, pos_bits, pbit, zero_row)
        return jnp.where(tied, cut, jnp.int32(2 ** 30))

    cut = lax.cond(any_tied, tie_cutoff, lambda: zero_row + jnp.int32(2 ** 30))

    def mask_tile(kt, carry):
        blk = keys_ref[tile_rows(kt), :]
        sel = jnp.logical_or(blk > thr, jnp.logical_and(blk == thr, pos_of(kt) <= cut))
        madd_ref[tile_rows(kt), :] = jnp.where(sel, 0.0, NEG_BIG).astype(f32)
        return carry

    lax.fori_loop(0, n_tiles, mask_tile, 0)

    stage_head_operands(q_ref, wq_ref)

    @pl.when(n_tiles % 2 == 1)
    def _():
        madd_ref[tile_rows(n_tiles), :] = jnp.full((QB, QB), NEG_BIG, f32)

    @pl.when(jnp.logical_and(pl.program_id(0) == 0, qi == 0))
    def _():
        p0_ref[...] = jnp.zeros_like(p0_ref)
        p1_ref[...] = jnp.zeros_like(p1_ref)

    l_refs = (l0_ref, l1_ref)
    p_refs = (p0_ref, p1_ref)
    last_tile = 2 * n_pairs - 1
    acc_rows = lambda h: slice(h * DSA_ACC_ROWS, (h + 1) * DSA_ACC_ROWS)
    ones_rows = jnp.ones((DSA_ACC_ROWS - DSA_DH, QB), bf16)

    def stage_logits(near, t, slot):
        if near:
            t = jnp.minimum(t, last_tile)
            rel = qi - t
            bias_idx = jnp.where(rel < 0, 2, jnp.minimum(rel, 2)) * DSA_HEADS
        for h in range(DSA_HEADS):
            kp = k_ref[tile_rows(t), (h // 2) * LANES:(h // 2 + 1) * LANES]
            l = _dot(kp, wq_ref[h]) + madd_ref[tile_rows(t), :]
            if near:
                l = l + bias_ref[bias_idx + h]
            l_refs[slot][h] = l
            ml_ref[slot * DSA_HEADS + h] = jnp.max(group_max(l), axis=0, keepdims=True)

    def stage_probs(slot, run_max):
        new_max = []
        for h in range(DSA_HEADS):
            r = jnp.maximum(run_max[h], ml_ref[slot * DSA_HEADS + h])
            p_refs[slot][h] = jnp.exp2(l_refs[slot][h] - r).astype(bf16)
            rt_ref[slot * DSA_HEADS + h] = r
            new_max.append(r)
        return tuple(new_max)

    def stage_pv(t, slot, prev_max):
        t = jnp.maximum(t, 0)
        new_max = []
        for h in range(DSA_HEADS):
            r = rt_ref[slot * DSA_HEADS + h]
            a = jnp.exp2(prev_max[h] - r)
            x = jnp.concatenate([vt_ref[h * DSA_DH:(h + 1) * DSA_DH, tile_rows(t)], ones_rows], axis=0)
            acc_ref[acc_rows(h), :] = acc_ref[acc_rows(h), :] * a + _dot(x, p_refs[slot][h])
            new_max.append(r)
        return tuple(new_max)

    acc_ref[...] = jnp.zeros_like(acc_ref)
    rt_ref[...] = jnp.full(rt_ref.shape, NEG_BIG, f32)
    stage_logits(True, 0, 0)
    stage_logits(True, 1, 1)

    def step(near, j, carry):
        probs_max, pv_max = carry
        pv_max = stage_pv(2 * j - 2, 0, pv_max)
        pv_max = stage_pv(2 * j - 1, 1, pv_max)
        probs_max = stage_probs(0, probs_max)
        probs_max = stage_probs(1, probs_max)
        stage_logits(near, 2 * j + 2, 0)
        stage_logits(near, 2 * j + 3, 1)
        return probs_max, pv_max

    neg_rows = tuple(jnp.full((1, QB), NEG_BIG, f32) for _ in range(DSA_HEADS))
    n_far_steps = jnp.maximum((qi - 3) // 2, 0)
    carry = lax.fori_loop(0, n_far_steps, functools.partial(step, False), (neg_rows, neg_rows))
    _, pv_max = lax.fori_loop(n_far_steps, n_pairs, functools.partial(step, True), carry)
    pv_max = stage_pv(2 * n_pairs - 2, 0, pv_max)
    stage_pv(2 * n_pairs - 1, 1, pv_max)
    outs = []
    for h in range(DSA_HEADS):
        acc = acc_ref[acc_rows(h), :]
        outs.append(acc[0:DSA_DH, :] * (1.0 / acc[DSA_DH:DSA_DH + 1, :]))
    o_ref[...] = jnp.concatenate(outs, axis=0).T.astype(o_ref.dtype)


def _dsa(dqk, vt, iq, ikk, iw, bias_tiles, B, S):
    T = dqk.shape[0]
    QB = DSA_QB
    nq = S // QB
    assert nq % 2 == 0
    topk = min(TOPK_MAX, S // 4)
    pos_bits = max(1, int(math.ceil(math.log2(S))))
    kern = functools.partial(_dsa_kernel, topk, pos_bits)
    return pl.pallas_call(
        kern,
        grid=(B, nq),
        in_specs=[
            pl.BlockSpec((QB, DSA_W), lambda b, i: (b * nq + i, 0)),
            pl.BlockSpec((S, DSA_W), lambda b, i: (b, 1)),
            pl.BlockSpec((None, DSA_W, S), lambda b, i: (b, 0, 0)),
            pl.BlockSpec((QB, IDX_Q), lambda b, i: (b * nq + i, 0)),
            pl.BlockSpec((S, LANES), lambda b, i: (b, 0)),
            pl.BlockSpec((QB, LANES), lambda b, i: (b * nq + i, 0)),
            _const_spec(bias_tiles.shape),
        ],
        out_specs=pl.BlockSpec((QB, DSA_W), lambda b, i: (b * nq + i, 0)),
        out_shape=jax.ShapeDtypeStruct((T, DSA_W), bf16),
        scratch_shapes=[pltpu.VMEM((S, QB), jnp.int32),
                        pltpu.VMEM((S, QB), jnp.int16),
                        pltpu.VMEM((S, QB), jnp.int16),
                        pltpu.VMEM((S, QB), f32),
                        pltpu.VMEM((DSA_HEADS * DSA_ACC_ROWS, QB), f32),
                        pltpu.VMEM((IDX_HEADS, LANES, QB), bf16),
                        pltpu.VMEM((DSA_HEADS, LANES, QB), bf16),
                        pltpu.VMEM((DSA_HEADS, QB, QB), f32),
                        pltpu.VMEM((DSA_HEADS, QB, QB), f32),
                        pltpu.VMEM((DSA_HEADS, QB, QB), bf16),
                        pltpu.VMEM((DSA_HEADS, QB, QB), bf16),
                        pltpu.VMEM((2 * DSA_HEADS, 1, QB), f32),
                        pltpu.VMEM((2 * DSA_HEADS, 1, QB), f32)],
        compiler_params=pltpu.CompilerParams(dimension_semantics=("arbitrary", "arbitrary"),
                                             vmem_limit_bytes=VMEM_LIMIT_BYTES),
        name="dsa",
    )(dqk, dqk, vt, iq, ikk, iw, bias_tiles)


def _rel_bucket(dist):
    max_exact = REL_BUCKETS // 2
    d = jnp.maximum(dist, 1).astype(f32)
    large = max_exact + (jnp.log(d / max_exact) / math.log(REL_MAX_DIST / max_exact)
                         * (REL_BUCKETS - max_exact)).astype(jnp.int32)
    large = jnp.minimum(large, REL_BUCKETS - 1)
    return jnp.where(dist < max_exact, dist, large)


def _bias_tiles(rel_bias):
    QB = DSA_QB
    j = jnp.arange(QB, dtype=jnp.int32)[:, None]
    i = jnp.arange(QB, dtype=jnp.int32)[None, :]
    far = rel_bias[REL_BUCKETS - 1].astype(f32)
    tiles = []
    assert 2 * QB - (QB - 1) >= REL_MAX_DIST
    for rel in range(3):
        dist = jnp.maximum(rel * QB + i - j, 0)
        onehot = (_rel_bucket(dist)[None] == jnp.arange(REL_BUCKETS, dtype=jnp.int32)[:, None, None])
        tiles.append(jnp.einsum('bh,bji->hji', (rel_bias.astype(f32) - far[None, :]) * LOG2E,
                                onehot.astype(f32), precision=lax.Precision.HIGHEST))
    return jnp.concatenate(tiles, axis=0)


def _mix_kernel(h_ref, oa_ref, ob_ref, gate_ref, wa_ref, wb_ref, wo_ref, gain_ref, out_ref):
    D = h_ref.shape[1]
    ya = _dot(oa_ref[...], wa_ref[...])
    yb = _dot(ob_ref[...], wb_ref[...])
    ga = jax.nn.sigmoid(gate_ref[:, 0:D].astype(f32))
    gb = jax.nn.sigmoid(gate_ref[:, D:2 * D].astype(f32))
    mixed = (ga * ya + gb * yb).astype(bf16)
    m = _dot(mixed, wo_ref[...])
    out_ref[...] = h_ref[...] + _rms(m, gain_ref[...])


def _mix(h2d, oa, ob, gates, wa, wb, wo, gain):
    T, D = h2d.shape
    tm = min(TOKEN_TILE, T)
    tok = lambda n: pl.BlockSpec((tm, n), lambda t: (t, 0))
    return pl.pallas_call(
        _mix_kernel,
        grid=(T // tm,),
        in_specs=[tok(D), tok(GLA_V), tok(DSA_W), tok(2 * D),
                  _const_spec(wa.shape), _const_spec(wb.shape), _const_spec(wo.shape),
                  _const_spec((1, D))],
        out_specs=tok(D),
        out_shape=jax.ShapeDtypeStruct((T, D), f32),
        compiler_params=pltpu.CompilerParams(dimension_semantics=("arbitrary",),
                                             vmem_limit_bytes=VMEM_LIMIT_BYTES),
        name="mix",
    )(h2d, oa, ob, gates, wa, wb, wo, gain)


def _mlp_kernel(h_ref, p_ref, w1_ref, w2_ref, wple_ref, wg_ref, g_pre_ref, g_post_ref, g_ple_ref,
                out_ref, acc_ref):
    h1 = h_ref[...]
    um = _rms(h1, g_pre_ref[...]).astype(bf16)
    d_ff = w1_ref.shape[1]
    for c in range(d_ff // FF_CHUNK):
        cs = slice(c * FF_CHUNK, (c + 1) * FF_CHUNK)
        hid = jnp.maximum(_dot(um, w1_ref[:, cs]), 0.0)
        part = _dot((hid * hid).astype(bf16), w2_ref[cs, :])
        if c == 0:
            acc_ref[...] = part
        else:
            acc_ref[...] += part
    h2 = h1 + _rms(acc_ref[...], g_post_ref[...])
    e = _dot(p_ref[...].astype(bf16), wple_ref[...]) * jax.nn.sigmoid(_dot(h2.astype(bf16), wg_ref[...]))
    out_ref[...] = h2 + _rms(e, g_ple_ref[...])


def _mlp(h2d, p2d, w1, w2, wple, wg, g_pre, g_post, g_ple):
    T, D = h2d.shape
    tm = min(TOKEN_TILE, T)
    tok = lambda n: pl.BlockSpec((tm, n), lambda t: (t, 0))
    return pl.pallas_call(
        _mlp_kernel,
        grid=(T // tm,),
        in_specs=[tok(D), tok(p2d.shape[1]),
                  _const_spec(w1.shape), _const_spec(w2.shape), _const_spec(wple.shape),
                  _const_spec(wg.shape), _const_spec((1, D)), _const_spec((1, D)), _const_spec((1, D))],
        out_specs=tok(D),
        out_shape=jax.ShapeDtypeStruct((T, D), f32),
        scratch_shapes=[pltpu.VMEM((tm, D), f32)],
        compiler_params=pltpu.CompilerParams(dimension_semantics=("arbitrary",),
                                             vmem_limit_bytes=VMEM_LIMIT_BYTES),
        name="mlp",
    )(h2d, p2d, w1, w2, wple, wg, g_pre, g_post, g_ple)


def _regroup_w_in(w, d_model):
    sizes = (GLA_QK, GLA_QK, GLA_V, GLA_V, GLA_GATE_RANK, DSA_W, DSA_W, DSA_W,
             IDX_Q, IDX_DIM, IDX_HEADS, d_model, d_model)
    pts = np.cumsum(sizes)[:-1].tolist()
    gq, gk, gv, gg, glr, dq, dk, dv, iq, ik, iw, ga, gb = jnp.split(w, pts, axis=1)
    pad = lambda a: jnp.pad(a, ((0, 0), (0, LANES - a.shape[1])))
    w_all = jnp.concatenate(
        [gq * (GLA_DK ** -0.5), gk, gv, gg, dq * (DSA_DH ** -0.5 * LOG2E), dk, iq, ik, ik, ga, gb,
         pad(glr), pad(iw)], axis=1).astype(bf16)
    assert w_all.shape[1] == _P_TOTAL
    return w_all, dv.T.astype(bf16)


def kernel(x, p, rel_bias, ln_mix_pre, ln_mix_post, ln_mlp_pre, ln_mlp_post, ln_ple_post, w_in,
           gla_gate_w2, gla_gate_b, gla_norm, w_branch_a, w_branch_b, w_out, w_mlp_in, w_mlp_out,
           w_ple, w_ple_gate):
    B, S, D = x.shape
    depth = w_in.shape[0]
    T = B * S
    assert S % DSA_QB == 0 and S % GLA_STEP == 0 and T % TOKEN_TILE == 0
    bias_tiles = _bias_tiles(rel_bias)
    h = x.reshape(T, D)
    row = lambda a: a.reshape(1, -1).astype(f32)
    for i in range(depth):
        w_all, w_vt = _regroup_w_in(w_in[i], D)
        w2p = jnp.pad(gla_gate_w2[i], ((0, LANES - GLA_GATE_RANK), (0, 0))).astype(bf16)
        gla_in, glog, dqk, vt, iq, ikk, iw, gates = _proj(
            h, row(ln_mix_pre[i]), w_all, w_vt, w2p, row(gla_gate_b[i]), B, S)
        o_a = _gla(gla_in, glog, row(gla_norm[i]), B, S)
        o_b = _dsa(dqk, vt, iq, ikk, iw, bias_tiles, B, S)
        h = _mix(h, o_a, o_b, gates, w_branch_a[i].astype(bf16), w_branch_b[i].astype(bf16),
                 w_out[i].astype(bf16), row(ln_mix_post[i]))
        h = _mlp(h, p[i].reshape(T, -1), w_mlp_in[i].astype(bf16), w_mlp_out[i].astype(bf16),
                 w_ple[i].astype(bf16), w_ple_gate[i].astype(bf16),
                 row(ln_mlp_pre[i]), row(ln_mlp_post[i]), row(ln_ple_post[i]))
    return h.reshape(B, S, D)
```
